```python
import jax, jax.numpy as jnp
from jax import lax
import numpy as np

D_MODEL = 1024
BATCH = 8
SEQ = 4096
DEPTH = 4

N_HEADS = 16
HEAD_DIM = D_MODEL // N_HEADS
GRID_W = 64
CTX_LEN = 256
WIN_H = 8
WIN_W = 16
QB_W = 16
KB_W = QB_W + WIN_W
CONV_K = 31
D_FF = -(-8 * D_MODEL // (3 * 256)) * 256
N_MIXERS = 2
ALPHA = (2 * DEPTH) ** 0.25
BETA = (8 * DEPTH) ** -0.25
LN_EPS = 1e-5
NEG_INF = -1e30

kernel_name = "hybrid_natten_conformer_diffusion_trunk"


def layer_norm(x, g, b):
    xf = x.astype(jnp.float32)
    mu = jnp.mean(xf, axis=-1, keepdims=True)
    var = jnp.mean(jnp.square(xf - mu), axis=-1, keepdims=True)
    y = (xf - mu) * lax.rsqrt(var + LN_EPS)
    return (y * g.astype(jnp.float32) + b.astype(jnp.float32)).astype(x.dtype)


def _column_structure():
    n_cb = GRID_W // QB_W
    qcol = np.arange(n_cb)[:, None] * QB_W + np.arange(QB_W)[None, :]
    qstart = np.clip(qcol - WIN_W // 2, 0, GRID_W - WIN_W)
    band0 = np.minimum(qstart[:, 0], GRID_W - KB_W)
    kcol = band0[:, None] + np.arange(KB_W)[None, :]
    mask = (kcol[:, None, :] >= qstart[:, :, None]) & (kcol[:, None, :] < qstart[:, :, None] + WIN_W)
    off = np.clip(kcol[:, None, :] - qcol[:, :, None] + WIN_W - 1, 0, 2 * WIN_W - 2)
    return kcol, mask, off


def neighborhood_attention(q, k, v, k_ctx, v_ctx, rpb):
    b, n, h, dh = q.shape
    rows = n // GRID_W
    kh = min(WIN_H, rows)
    n_cb = GRID_W // QB_W
    kcol, cmask, coff = _column_structure()
    kcol = jnp.asarray(kcol.reshape(-1), dtype=jnp.int32)
    cmask = jnp.asarray(cmask)
    coff = jnp.asarray(coff, dtype=jnp.int32)
    k_grid = k.reshape(b, rows, GRID_W, h, dh)
    v_grid = v.reshape(b, rows, GRID_W, h, dh)
    q_rows = jnp.moveaxis(q.reshape(b, rows, n_cb, QB_W, h, dh), 1, 0)
    scale = HEAD_DIM ** -0.5
    n_loc = kh * KB_W

    def one_row(args):
        r, q_r = args
        r0 = jnp.clip(r - kh // 2, 0, rows - kh)
        k_r = lax.dynamic_slice_in_dim(k_grid, r0, kh, axis=1)
        v_r = lax.dynamic_slice_in_dim(v_grid, r0, kh, axis=1)
        k_band = jnp.take(k_r, kcol, axis=2).reshape(b, kh, n_cb, KB_W, h, dh)
        v_band = jnp.take(v_r, kcol, axis=2).reshape(b, kh, n_cb, KB_W, h, dh)
        row_off = r0 + jnp.arange(kh, dtype=jnp.int32) - r + (WIN_H - 1)
        bias = rpb[:, row_off[None, None, :, None], coff[:, :, None, :]]
        s_loc = jnp.einsum('bnqhd,binmhd->bhnqim', q_r, k_band,
                           preferred_element_type=jnp.float32) * scale + bias.astype(jnp.float32)
        s_loc = jnp.where(cmask[:, :, None, :], s_loc, NEG_INF)
        s_ctx = jnp.einsum('bnqhd,bchd->bhnqc', q_r, k_ctx,
                           preferred_element_type=jnp.float32) * scale
        s = jnp.concatenate([s_loc.reshape(b, h, n_cb, QB_W, n_loc), s_ctx], axis=-1)
        p = jax.nn.softmax(s, axis=-1).astype(v.dtype)
        p_loc = p[..., :n_loc].reshape(b, h, n_cb, QB_W, kh, KB_W)
        p_ctx = p[..., n_loc:]
        return (jnp.einsum('bhnqim,binmhd->bnqhd', p_loc, v_band)
                + jnp.einsum('bhnqc,bchd->bnqhd', p_ctx, v_ctx))

    out = lax.map(one_row, (jnp.arange(rows, dtype=jnp.int32), q_rows))
    return jnp.moveaxis(out, 0, 1).reshape(b, n, h * dh)


def attn_mixer(h_lat, h_ctx, w_qkv, b_qkv, w_o, b_o, rpb, ctx_out):
    b, n, d = h_lat.shape
    cl = h_ctx.shape[1]
    qkv = h_lat @ w_qkv + b_qkv
    q_l, k_l, v_l = [t.reshape(b, n, N_HEADS, HEAD_DIM) for t in jnp.split(qkv, 3, axis=-1)]
    if ctx_out:
        qkv_c = h_ctx @ w_qkv + b_qkv
        q_c, k_c, v_c = [t.reshape(b, cl, N_HEADS, HEAD_DIM) for t in jnp.split(qkv_c, 3, axis=-1)]
    else:
        kv_c = h_ctx @ w_qkv[:, d:] + b_qkv[d:]
        k_c, v_c = [t.reshape(b, cl, N_HEADS, HEAD_DIM) for t in jnp.split(kv_c, 2, axis=-1)]
    y_l = neighborhood_attention(q_l, k_l, v_l, k_c, v_c, rpb) @ w_o + b_o
    if not ctx_out:
        return y_l, None
    s = jnp.einsum('bqhd,bkhd->bhqk', q_c, k_c, preferred_element_type=jnp.float32) * HEAD_DIM ** -0.5
    p = jax.nn.softmax(s, axis=-1).astype(v_c.dtype)
    o_c = jnp.einsum('bhqk,bkhd->bqhd', p, v_c).reshape(b, cl, d)
    return y_l, o_c @ w_o + b_o


def conformer_conv(h, w_pw1, b_pw1, w_dw, b_dw, ln_g, ln_b, w_pw2, b_pw2):
    d = h.shape[-1]
    a, g = jnp.split(h @ w_pw1 + b_pw1, 2, axis=-1)
    u = a * jax.nn.sigmoid(g)
    pad = CONV_K // 2
    u = lax.conv_general_dilated(u, w_dw[:, None, :], window_strides=(1,), padding=((pad, pad),),
                                 dimension_numbers=('NWC', 'WIO', 'NWC'), feature_group_count=d) + b_dw
    u = jax.nn.silu(layer_norm(u, ln_g, ln_b))
    return u @ w_pw2 + b_pw2


def swiglu(h, w1, w3, w2):
    return (jax.nn.silu(h @ w1) * (h @ w3)) @ w2


def setup_inputs(seed: int = 0) -> dict:
    key = jax.random.key(seed)
    ks = jax.random.split(key, 26)
    n_a = len([i for i in range(DEPTH) if i % N_MIXERS == 0])
    n_b = DEPTH - n_a
    D = D_MODEL

    def nrm(k, shape, scale):
        return jax.random.normal(k, shape, jnp.float32) * scale

    def gain(k, shape):
        return 1.0 + 0.02 * jax.random.normal(k, shape, jnp.float32)

    return {
        "x": nrm(ks[0], (BATCH, SEQ, D), 1.0),
        "c": nrm(ks[1], (BATCH, D), 1.0),
        "ctx": nrm(ks[2], (BATCH, CTX_LEN, D), 1.0),
        "c_ctx": nrm(ks[3], (D,), 1.0),
        "w_ada": nrm(ks[4], (DEPTH, D, 6 * D), D ** -0.5),
        "b_ada": nrm(ks[5], (DEPTH, 6 * D), 0.01),
        "ln_mix_g": gain(ks[6], (DEPTH, D)),
        "ln_mix_b": nrm(ks[7], (DEPTH, D), 0.01),
        "ln_ffn_g": gain(ks[8], (DEPTH, D)),
        "ln_ffn_b": nrm(ks[9], (DEPTH, D), 0.01),
        "attn_w_qkv": nrm(ks[10], (n_a, D, 3 * D), D ** -0.5),
        "attn_b_qkv": nrm(ks[11], (n_a, 3 * D), 0.01),
        "attn_w_o": nrm(ks[12], (n_a, D, D), BETA * D ** -0.5),
        "attn_b_o": nrm(ks[13], (n_a, D), 0.01),
        "attn_rpb": nrm(ks[14], (n_a, N_HEADS, 2 * WIN_H - 1, 2 * WIN_W - 1), 0.1),
        "conv_w_pw1": nrm(ks[15], (n_b, D, 2 * D), D ** -0.5),
        "conv_b_pw1": nrm(ks[16], (n_b, 2 * D), 0.01),
        "conv_w_dw": nrm(ks[17], (n_b, CONV_K, D), CONV_K ** -0.5),
        "conv_b_dw": nrm(ks[18], (n_b, D), 0.01),
        "conv_ln_g": gain(ks[19], (n_b, D)),
        "conv_ln_b": nrm(ks[20], (n_b, D), 0.01),
        "conv_w_pw2": nrm(ks[21], (n_b, D, D), BETA * D ** -0.5),
        "conv_b_pw2": nrm(ks[22], (n_b, D), 0.01),
        "ffn_w1": nrm(ks[23], (DEPTH, D, D_FF), D ** -0.5),
        "ffn_w3": nrm(ks[24], (DEPTH, D, D_FF), D ** -0.5),
        "ffn_w2": nrm(ks[25], (DEPTH, D_FF, D), BETA * D_FF ** -0.5),
    }


def reference(x, c, ctx, c_ctx, w_ada, b_ada, ln_mix_g, ln_mix_b, ln_ffn_g, ln_ffn_b,
              attn_w_qkv, attn_b_qkv, attn_w_o, attn_b_o, attn_rpb,
              conv_w_pw1, conv_b_pw1, conv_w_dw, conv_b_dw, conv_ln_g, conv_ln_b, conv_w_pw2, conv_b_pw2,
              ffn_w1, ffn_w3, ffn_w2):
    last_attn = max(i for i in range(DEPTH) if i % N_MIXERS == 0)
    s_lat = jax.nn.silu(c)
    s_ctx = jax.nn.silu(c_ctx)[None, :]
    xc = ctx
    for i in range(DEPTH):
        slot = i // N_MIXERS
        ctx_in = i <= last_attn
        ctx_live = i < last_attn
        sh1, sc1, g1, sh2, sc2, g2 = jnp.split((s_lat @ w_ada[i] + b_ada[i])[:, None, :], 6, axis=-1)
        h = x * (1 + sc1) + sh1
        hc = None
        if ctx_in:
            csh1, csc1, cg1, csh2, csc2, cg2 = jnp.split((s_ctx @ w_ada[i] + b_ada[i])[:, None, :], 6, axis=-1)
            hc = xc * (1 + csc1) + csh1
        if i % N_MIXERS == 0:
            y, yc = attn_mixer(h, hc, attn_w_qkv[slot], attn_b_qkv[slot], attn_w_o[slot], attn_b_o[slot],
                               attn_rpb[slot], ctx_live)
        else:
            conv_args = (conv_w_pw1[slot], conv_b_pw1[slot], conv_w_dw[slot], conv_b_dw[slot],
                         conv_ln_g[slot], conv_ln_b[slot], conv_w_pw2[slot], conv_b_pw2[slot])
            y = conformer_conv(h, *conv_args)
            yc = conformer_conv(hc, *conv_args) if ctx_live else None
        x = layer_norm(ALPHA * x + g1 * y, ln_mix_g[i], ln_mix_b[i])
        h = x * (1 + sc2) + sh2
        x = layer_norm(ALPHA * x + g2 * swiglu(h, ffn_w1[i], ffn_w3[i], ffn_w2[i]), ln_ffn_g[i], ln_ffn_b[i])
        if ctx_live:
            xc = layer_norm(ALPHA * xc + cg1 * yc, ln_mix_g[i], ln_mix_b[i])
            hc = xc * (1 + csc2) + csh2
            xc = layer_norm(ALPHA * xc + cg2 * swiglu(hc, ffn_w1[i], ffn_w3[i], ffn_w2[i]),
                            ln_ffn_g[i], ln_ffn_b[i])
    return x
```

```python
import functools

import numpy as np
import jax
import jax.numpy as jnp
from jax import lax
from jax.experimental import pallas as pl
from jax.experimental.pallas import tpu as pltpu

F32 = jnp.float32
BF16 = jnp.bfloat16

N_HEADS = 16
HEAD_DIM = 64
GRID_W = 64
WIN_H = 8
WIN_W = 16
CONV_K = 31
N_MIXERS = 2
LN_EPS = 1e-5
NEG_INF = -1e30

Q_ROWS = 8
KEY_ROWS = Q_ROWS + WIN_H
HEADS_PER_STEP = 2
HALO = 16
VMEM_LIMIT = 56 * 1024 * 1024


def _params(n_axes):
    return pltpu.CompilerParams(dimension_semantics=("arbitrary",) * n_axes,
                                vmem_limit_bytes=VMEM_LIMIT)


def _resident(shape):
    return pl.BlockSpec(shape, lambda *_: (0,) * len(shape), pipeline_mode=pl.Buffered(1))


def _layer_norm(z, g, b):
    mu = jnp.mean(z, axis=-1, keepdims=True)
    zc = z - mu
    var = jnp.mean(zc * zc, axis=-1, keepdims=True)
    return zc * lax.rsqrt(var + LN_EPS) * g + b


def _modulate(x, sc, sh):
    return (x * (1.0 + sc) + sh).astype(BF16)


def _dot(a, b):
    return jnp.dot(a, b, preferred_element_type=F32)


def _dot_nt(a, b):
    return lax.dot_general(a, b, (((1,), (1,)), ((), ())), preferred_element_type=F32)


def _ada_kernel(s_ref, w_ref, b_ref, o_ref):
    s = s_ref[...]
    s = s * jax.nn.sigmoid(s)
    o_ref[...] = _dot(s.astype(BF16), w_ref[...].astype(BF16)) + b_ref[...]


def _ada_all(s_rows, w_ada, b_ada, tn=1536):
    depth, d, n = w_ada.shape
    rows = s_rows.shape[0]
    return pl.pallas_call(
        _ada_kernel,
        grid=(depth, n // tn),
        in_specs=[pl.BlockSpec((rows, d), lambda i, j: (0, 0)),
                  pl.BlockSpec((None, d, tn), lambda i, j: (i, 0, j)),
                  pl.BlockSpec((None, 1, tn), lambda i, j: (i, 0, j))],
        out_specs=pl.BlockSpec((None, rows, tn), lambda i, j: (i, 0, j)),
        out_shape=jax.ShapeDtypeStruct((depth, rows, n), F32),
        compiler_params=_params(2),
        name="ada_params",
    )(s_rows, w_ada, b_ada.reshape(depth, 1, n))


def _qkv_kernel(x_ref, sc_ref, sh_ref, w_ref, b_ref, q_ref, k_ref, v_ref, *, nc):
    d = x_ref.shape[-1]
    h = _modulate(x_ref[...], sc_ref[...], sh_ref[...])
    for idx, o_ref in enumerate((q_ref, k_ref, v_ref)):
        for n in range(d // nc):
            col = idx * d + n * nc
            y = _dot(h, w_ref[:, col:col + nc]) + b_ref[:, col:col + nc]
            if idx == 0:
                y = y * (HEAD_DIM ** -0.5)
            o_ref[:, n * nc:(n + 1) * nc] = y.astype(BF16)


def _qkv(x, sc, sh, w, b, tm):
    bsz, l, d = x.shape
    tok = pl.BlockSpec((None, tm, d), lambda i, j: (i, j, 0))
    vec = pl.BlockSpec((None, 1, d), lambda i, j: (i, 0, 0))
    out = jax.ShapeDtypeStruct((bsz, l, d), BF16)
    return pl.pallas_call(
        functools.partial(_qkv_kernel, nc=512),
        grid=(bsz, l // tm),
        in_specs=[tok, vec, vec, _resident((d, 3 * d)), _resident((1, 3 * d))],
        out_specs=[tok, tok, tok],
        out_shape=[out, out, out],
        compiler_params=_params(2),
        name="modulate_qkv",
    )(x, sc, sh, w, b.reshape(1, 3 * d))


def _bias_patterns(rows):
    last = rows - Q_ROWS
    return ((0, 0), (Q_ROWS, Q_ROWS - WIN_H // 2), (last, rows - KEY_ROWS))


def _bias_kernel(t_ref, o_ref, *, rows):
    p = pl.program_id(1)
    for pat, (qrow0, w0) in enumerate(_bias_patterns(rows)):
        @pl.when(p == pat)
        def _():
            for i in range(Q_ROWS):
                r = qrow0 + i
                r0 = min(max(r - WIN_H // 2, 0), rows - WIN_H)
                for j in range(KEY_ROWS):
                    kr = w0 + j
                    lane0 = (j % 2) * GRID_W
                    dst = (slice(i * GRID_W, (i + 1) * GRID_W), slice(j * GRID_W, (j + 1) * GRID_W))
                    if r0 <= kr < r0 + WIN_H:
                        o_ref[dst] = t_ref[kr - r + WIN_H - 1, :, lane0:lane0 + GRID_W]
                    else:
                        o_ref[dst] = jnp.full((GRID_W, GRID_W), NEG_INF, F32)


def _bias_table(rpb, rows):
    nh = rpb.shape[0]
    qc = np.arange(GRID_W)[:, None]
    kc = np.arange(GRID_W)[None, :]
    qstart = np.clip(qc - WIN_W // 2, 0, GRID_W - WIN_W)
    col_ok = (kc >= qstart) & (kc < qstart + WIN_W)
    off = np.clip(kc - qc + WIN_W - 1, 0, 2 * WIN_W - 2)
    toep = jnp.where(jnp.asarray(col_ok), rpb[:, :, off], NEG_INF)
    toep = jnp.concatenate([toep, toep], axis=-1)
    nd = 2 * WIN_H - 1
    return pl.pallas_call(
        functools.partial(_bias_kernel, rows=rows),
        grid=(nh, 3),
        in_specs=[pl.BlockSpec((None, nd, GRID_W, 2 * GRID_W), lambda h, p: (h, 0, 0, 0))],
        out_specs=pl.BlockSpec((None, None, Q_ROWS * GRID_W, KEY_ROWS * GRID_W), lambda h, p: (p, h, 0, 0)),
        out_shape=jax.ShapeDtypeStruct((3, nh, Q_ROWS * GRID_W, KEY_ROWS * GRID_W), F32),
        compiler_params=_params(2),
        name="attn_bias_table",
    )(toep)


def _nattn_kernel(q_ref, k_ref, v_ref, kc_ref, vc_ref, bias_ref, o_ref, *, rows):
    g = pl.program_id(2)
    n_groups = rows // Q_ROWS
    pat = jnp.where(g == 0, 0, jnp.where(g == n_groups - 1, 2, 1))
    w0 = jnp.clip(g * Q_ROWS - WIN_H // 2, 0, rows - KEY_ROWS)
    start = pl.multiple_of(w0 * GRID_W, (WIN_H // 2) * GRID_W)
    n_keys = KEY_ROWS * GRID_W
    outs = []
    for h in range(HEADS_PER_STEP):
        sl = slice(h * HEAD_DIM, (h + 1) * HEAD_DIM)
        q = q_ref[:, sl]
        s = _dot_nt(q, k_ref[pl.ds(start, n_keys), sl]) + bias_ref[pat, h]
        sc = _dot_nt(q, kc_ref[:, sl])
        m = jnp.maximum(jnp.max(s, axis=-1, keepdims=True), jnp.max(sc, axis=-1, keepdims=True))
        p = jnp.exp(s - m)
        pc = jnp.exp(sc - m)
        denom = jnp.sum(p, axis=-1, keepdims=True) + jnp.sum(pc, axis=-1, keepdims=True)
        o = _dot(p.astype(BF16), v_ref[pl.ds(start, n_keys), sl]) + _dot(pc.astype(BF16), vc_ref[:, sl])
        outs.append(o / denom)
    o_ref[...] = jnp.concatenate(outs, axis=-1).astype(BF16)


def _nattn(q, k, v, kc, vc, bias):
    bsz, n, d = q.shape
    cl = kc.shape[1]
    rows = n // GRID_W
    lanes = HEADS_PER_STEP * HEAD_DIM
    tq = Q_ROWS * GRID_W
    qspec = pl.BlockSpec((None, tq, lanes), lambda hp, b, g: (b, g, hp))
    kvspec = pl.BlockSpec((None, n, lanes), lambda hp, b, g: (b, 0, hp))
    cspec = pl.BlockSpec((None, cl, lanes), lambda hp, b, g: (b, 0, hp))
    bspec = pl.BlockSpec((3, HEADS_PER_STEP, tq, KEY_ROWS * GRID_W), lambda hp, b, g: (0, hp, 0, 0),
                         pipeline_mode=pl.Buffered(1))
    return pl.pallas_call(
        functools.partial(_nattn_kernel, rows=rows),
        grid=(d // lanes, bsz, rows // Q_ROWS),
        in_specs=[qspec, kvspec, kvspec, cspec, cspec, bspec],
        out_specs=qspec,
        out_shape=jax.ShapeDtypeStruct((bsz, n, d), BF16),
        compiler_params=_params(3),
        name="neighborhood_attention",
    )(q, k, v, kc, vc, bias)


def _cattn_kernel(q_ref, k_ref, v_ref, o_ref):
    outs = []
    for h in range(HEADS_PER_STEP):
        sl = slice(h * HEAD_DIM, (h + 1) * HEAD_DIM)
        s = _dot_nt(q_ref[:, sl], k_ref[:, sl])
        p = jnp.exp(s - jnp.max(s, axis=-1, keepdims=True))
        denom = jnp.sum(p, axis=-1, keepdims=True)
        outs.append(_dot(p.astype(BF16), v_ref[:, sl]) / denom)
    o_ref[...] = jnp.concatenate(outs, axis=-1).astype(BF16)


def _cattn(q, k, v):
    bsz, cl, d = q.shape
    lanes = HEADS_PER_STEP * HEAD_DIM
    spec = pl.BlockSpec((None, cl, lanes), lambda b, hp: (b, 0, hp))
    return pl.pallas_call(
        _cattn_kernel,
        grid=(bsz, d // lanes),
        in_specs=[spec, spec, spec],
        out_specs=spec,
        out_shape=jax.ShapeDtypeStruct((bsz, cl, d), BF16),
        compiler_params=_params(2),
        name="context_attention",
    )(q, k, v)


def _proj_ln_kernel(o_ref, w_ref, b_ref, x_ref, g_ref, lg_ref, lb_ref, out_ref, *, alpha):
    y = _dot(o_ref[...], w_ref[...]) + b_ref[...]
    z = alpha * x_ref[...] + g_ref[...] * y
    out_ref[...] = _layer_norm(z, lg_ref[...], lb_ref[...])


def _proj_ln(o, w, b, x, gate, ln_g, ln_b, alpha, tm):
    bsz, l, d = x.shape
    tok = pl.BlockSpec((None, tm, d), lambda i, j: (i, j, 0))
    vec = pl.BlockSpec((None, 1, d), lambda i, j: (i, 0, 0))
    return pl.pallas_call(
        functools.partial(_proj_ln_kernel, alpha=alpha),
        grid=(bsz, l // tm),
        in_specs=[tok, _resident((d, d)), _resident((1, d)), tok, vec, _resident((1, d)), _resident((1, d))],
        out_specs=tok,
        out_shape=jax.ShapeDtypeStruct((bsz, l, d), F32),
        compiler_params=_params(2),
        name="out_proj_deepnorm",
    )(o, w, b.reshape(1, d), x, gate, ln_g.reshape(1, d), ln_b.reshape(1, d))


def _ffn_chunks(d_ff, chunk=512):
    edges = list(range(0, d_ff, chunk)) + [d_ff]
    return list(zip(edges[:-1], edges[1:]))


def _ffn_kernel(x_ref, sc_ref, sh_ref, g_ref, w1_ref, w3_ref, w2_ref, lg_ref, lb_ref, out_ref, *, alpha):
    x = x_ref[...]
    h = _modulate(x, sc_ref[...], sh_ref[...])
    acc = None
    for lo, hi in _ffn_chunks(w1_ref.shape[1]):
        a = _dot(h, w1_ref[:, lo:hi])
        b = _dot(h, w3_ref[:, lo:hi])
        act = (a * jax.nn.sigmoid(a) * b).astype(BF16)
        part = _dot(act, w2_ref[lo:hi, :])
        acc = part if acc is None else acc + part
    z = alpha * x + g_ref[...] * acc
    out_ref[...] = _layer_norm(z, lg_ref[...], lb_ref[...])


def _ffn(x, sc, sh, gate, w1, w3, w2, ln_g, ln_b, alpha, tm):
    bsz, l, d = x.shape
    d_ff = w1.shape[1]
    tok = pl.BlockSpec((None, tm, d), lambda i, j: (i, j, 0))
    vec = pl.BlockSpec((None, 1, d), lambda i, j: (i, 0, 0))
    return pl.pallas_call(
        functools.partial(_ffn_kernel, alpha=alpha),
        grid=(bsz, l // tm),
        in_specs=[tok, vec, vec, vec, _resident((d, d_ff)), _resident((d, d_ff)), _resident((d_ff, d)),
                  _resident((1, d)), _resident((1, d))],
        out_specs=tok,
        out_shape=jax.ShapeDtypeStruct((bsz, l, d), F32),
        compiler_params=_params(2),
        name="swiglu_deepnorm",
    )(x, sc, sh, gate, w1, w3, w2, ln_g.reshape(1, d), ln_b.reshape(1, d))


def _glu_kernel(x_ref, sc_ref, sh_ref, w_ref, b_ref, u_ref, *, nc):
    d = x_ref.shape[-1]
    h = _modulate(x_ref[...], sc_ref[...], sh_ref[...])
    for n in range(d // nc):
        lo = n * nc
        a = _dot(h, w_ref[:, lo:lo + nc]) + b_ref[:, lo:lo + nc]
        g = _dot(h, w_ref[:, d + lo:d + lo + nc]) + b_ref[:, d + lo:d + lo + nc]
        u_ref[:, lo:lo + nc] = a * jax.nn.sigmoid(g)


def _glu(x, sc, sh, w, b, tm):
    bsz, l, d = x.shape
    tok = pl.BlockSpec((None, tm, d), lambda i, j: (i, j, 0))
    vec = pl.BlockSpec((None, 1, d), lambda i, j: (i, 0, 0))
    return pl.pallas_call(
        functools.partial(_glu_kernel, nc=512),
        grid=(bsz, l // tm),
        in_specs=[tok, vec, vec, _resident((d, 2 * d)), _resident((1, 2 * d))],
        out_specs=tok,
        out_shape=jax.ShapeDtypeStruct((bsz, l, d), F32),
        compiler_params=_params(2),
        name="modulate_pw1_glu",
    )(x, sc, sh, w, b.reshape(1, 2 * d))


def _conv_kernel(up_ref, u_ref, un_ref, wdw_ref, bdw_ref, cg_ref, cb_ref, w2_ref, b2_ref,
                 x_ref, g_ref, lg_ref, lb_ref, out_ref, ext_ref, y_ref, *, alpha, rc, lc):
    j = pl.program_id(1)
    tm, d = u_ref.shape
    pad = CONV_K // 2
    ext_ref[0:HALO, :] = jnp.where(j == 0, 0.0, up_ref[...])
    ext_ref[HALO:HALO + tm, :] = u_ref[...]
    ext_ref[HALO + tm:HALO + tm + HALO, :] = jnp.where(j == pl.num_programs(1) - 1, 0.0, un_ref[...])

    span = HALO - pad + CONV_K - 1
    span8 = -(-span // 8) * 8

    def row_chunk(c, carry):
        r0 = pl.multiple_of(c * rc, rc)
        for l0 in range(0, d, lc):
            win = ext_ref[pl.ds(r0, rc + span8), l0:l0 + lc]
            acc = jnp.zeros((rc, lc), F32)
            for r in range(8):
                shifted = win[r:r + rc + span8 - 8, :]
                for a in range(span8 // 8):
                    k = 8 * a + r - (HALO - pad)
                    if 0 <= k < CONV_K:
                        acc = acc + shifted[8 * a:8 * a + rc, :] * wdw_ref[k:k + 1, l0:l0 + lc]
            y_ref[pl.ds(r0, rc), l0:l0 + lc] = acc
        return carry

    lax.fori_loop(0, tm // rc, row_chunk, 0)
    t = _layer_norm(y_ref[...] + bdw_ref[...], cg_ref[...], cb_ref[...])
    t = (t * jax.nn.sigmoid(t)).astype(BF16)
    y = _dot(t, w2_ref[...]) + b2_ref[...]
    z = alpha * x_ref[...] + g_ref[...] * y
    out_ref[...] = _layer_norm(z, lg_ref[...], lb_ref[...])


def _conv(u, w_dw, b_dw, cg, cb, w2, b2, x, gate, ln_g, ln_b, alpha, tm):
    bsz, l, d = x.shape
    nh = tm // HALO
    last = l // HALO - 1
    tok = pl.BlockSpec((None, tm, d), lambda i, j: (i, j, 0))
    prev = pl.BlockSpec((None, HALO, d), lambda i, j: (i, jnp.maximum(j * nh - 1, 0), 0))
    nxt = pl.BlockSpec((None, HALO, d), lambda i, j: (i, jnp.minimum((j + 1) * nh, last), 0))
    vec = pl.BlockSpec((None, 1, d), lambda i, j: (i, 0, 0))
    row = _resident((1, d))
    return pl.pallas_call(
        functools.partial(_conv_kernel, alpha=alpha, rc=64, lc=128),
        grid=(bsz, l // tm),
        in_specs=[prev, tok, nxt, _resident((CONV_K, d)), row, row, row, _resident((d, d)), row,
                  tok, vec, row, row],
        out_specs=tok,
        out_shape=jax.ShapeDtypeStruct((bsz, l, d), F32),
        scratch_shapes=[pltpu.VMEM((tm + 2 * HALO, d), F32), pltpu.VMEM((tm, d), F32)],
        compiler_params=_params(2),
        name="dwconv_pw2_deepnorm",
    )(u, u, u, w_dw, b_dw.reshape(1, d), cg.reshape(1, d), cb.reshape(1, d), w2, b2.reshape(1, d),
      x, gate, ln_g.reshape(1, d), ln_b.reshape(1, d))


def kernel(x, c, ctx, c_ctx, w_ada, b_ada, ln_mix_g, ln_mix_b, ln_ffn_g, ln_ffn_b, attn_w_qkv, attn_b_qkv, attn_w_o, attn_b_o, attn_rpb, conv_w_pw1, conv_b_pw1, conv_w_dw, conv_b_dw, conv_ln_g, conv_ln_b, conv_w_pw2, conv_b_pw2, ffn_w1, ffn_w3, ffn_w2):
    bsz, n, d = x.shape
    cl = ctx.shape[1]
    depth = w_ada.shape[0]
    rows = n // GRID_W
    assert d == N_HEADS * HEAD_DIM and n % (Q_ROWS * GRID_W) == 0 and rows >= KEY_ROWS
    alpha = (2 * depth) ** 0.25
    last_attn = max(i for i in range(depth) if i % N_MIXERS == 0)
    tm = min(512, n)
    tmc = min(512, cl)

    s_rows = jnp.zeros((16, d), F32).at[:bsz].set(c).at[bsz].set(c_ctx)
    ada = _ada_all(s_rows, w_ada, b_ada)

    def mods(i):
        lat = [ada[i, :bsz, k * d:(k + 1) * d].reshape(bsz, 1, d) for k in range(6)]
        cx = [jnp.broadcast_to(ada[i, bsz, k * d:(k + 1) * d].reshape(1, 1, d), (bsz, 1, d)) for k in range(6)]
        return lat, cx

    bf = lambda w: w.astype(BF16)
    xc = ctx
    for i in range(depth):
        slot = i // N_MIXERS
        ctx_in = i <= last_attn
        ctx_live = i < last_attn
        (sh1, sc1, g1, sh2, sc2, g2), (csh1, csc1, cg1, csh2, csc2, cg2) = mods(i)
        w1, w3, w2 = bf(ffn_w1[i]), bf(ffn_w3[i]), bf(ffn_w2[i])
        if i % N_MIXERS == 0:
            wqkv, wo = bf(attn_w_qkv[slot]), bf(attn_w_o[slot])
            q, k, v = _qkv(x, sc1, sh1, wqkv, attn_b_qkv[slot], tm)
            qc, kc, vc = _qkv(xc, csc1, csh1, wqkv, attn_b_qkv[slot], tmc)
            bias = _bias_table(attn_rpb[slot], rows)
            o = _nattn(q, k, v, kc, vc, bias)
            x = _proj_ln(o, wo, attn_b_o[slot], x, g1, ln_mix_g[i], ln_mix_b[i], alpha, tm)
            if ctx_live:
                oc = _cattn(qc, kc, vc)
                xc = _proj_ln(oc, wo, attn_b_o[slot], xc, cg1, ln_mix_g[i], ln_mix_b[i], alpha, tmc)
        else:
            wp1, wp2 = bf(conv_w_pw1[slot]), bf(conv_w_pw2[slot])
            cargs = (conv_w_dw[slot], conv_b_dw[slot], conv_ln_g[slot], conv_ln_b[slot], wp2, conv_b_pw2[slot])
            u = _glu(x, sc1, sh1, wp1, conv_b_pw1[slot], tm)
            x = _conv(u, *cargs, x, g1, ln_mix_g[i], ln_mix_b[i], alpha, tm)
            if ctx_live:
                uc = _glu(xc, csc1, csh1, wp1, conv_b_pw1[slot], tmc)
                xc = _conv(uc, *cargs, xc, cg1, ln_mix_g[i], ln_mix_b[i], alpha, tmc)
        x = _ffn(x, sc2, sh2, g2, w1, w3, w2, ln_ffn_g[i], ln_ffn_b[i], alpha, tm)
        if ctx_live:
            xc = _ffn(xc, csc2, csh2, cg2, w1, w3, w2, ln_ffn_g[i], ln_ffn_b[i], alpha, tmc)
    return x
```

```python
import functools

import numpy as np
import jax
import jax.numpy as jnp
from jax import lax
from jax.experimental import pallas as pl
from jax.experimental.pallas import tpu as pltpu

F32 = jnp.float32
BF16 = jnp.bfloat16

N_HEADS = 16
HEAD_DIM = 64
GRID_W = 64
WIN_H = 8
WIN_W = 16
CONV_K = 31
N_MIXERS = 2
LN_EPS = 1e-5
NEG_INF = -1e30

LOG2E = 1.4426950408889634
Q_ROWS = 4
KEY_ROWS = Q_ROWS + WIN_H
GROUPS_PER_STEP = 4
HEADS_PER_STEP = 2
HALO = 16
VMEM_LIMIT = 56 * 1024 * 1024


def _params(n_axes):
    return pltpu.CompilerParams(dimension_semantics=("arbitrary",) * n_axes,
                                vmem_limit_bytes=VMEM_LIMIT)


def _resident(shape):
    return pl.BlockSpec(shape, lambda *_: (0,) * len(shape), pipeline_mode=pl.Buffered(1))


def _layer_norm(z, g, b):
    mu = jnp.mean(z, axis=-1, keepdims=True)
    zc = z - mu
    var = jnp.mean(zc * zc, axis=-1, keepdims=True)
    return zc * lax.rsqrt(var + LN_EPS) * g + b


def _modulate(x, sc, sh):
    return (x * (1.0 + sc) + sh).astype(BF16)


def _dot(a, b):
    return jnp.dot(a, b, preferred_element_type=F32)


def _dot_nt(a, b):
    return lax.dot_general(a, b, (((1,), (1,)), ((), ())), preferred_element_type=F32)


def _ada_kernel(s_ref, w_ref, b_ref, o_ref):
    s = s_ref[...]
    s = s * jax.nn.sigmoid(s)
    o_ref[...] = _dot(s.astype(BF16), w_ref[...].astype(BF16)) + b_ref[...]


def _ada_all(s_rows, w_ada, b_ada, tn=1536):
    depth, d, n = w_ada.shape
    rows = s_rows.shape[0]
    return pl.pallas_call(
        _ada_kernel,
        grid=(depth, n // tn),
        in_specs=[pl.BlockSpec((rows, d), lambda i, j: (0, 0)),
                  pl.BlockSpec((None, d, tn), lambda i, j: (i, 0, j)),
                  pl.BlockSpec((None, 1, tn), lambda i, j: (i, 0, j))],
        out_specs=pl.BlockSpec((None, rows, tn), lambda i, j: (i, 0, j)),
        out_shape=jax.ShapeDtypeStruct((depth, rows, n), F32),
        compiler_params=_params(2),
        name="ada_params",
    )(s_rows, w_ada, b_ada.reshape(depth, 1, n))


def _qkv_kernel(x_ref, sc_ref, sh_ref, w_ref, b_ref, q_ref, k_ref, v_ref, *, nc):
    d = x_ref.shape[-1]
    h = _modulate(x_ref[...], sc_ref[...], sh_ref[...])
    for idx, o_ref in enumerate((q_ref, k_ref, v_ref)):
        for n in range(d // nc):
            col = idx * d + n * nc
            y = _dot(h, w_ref[:, col:col + nc]) + b_ref[:, col:col + nc]
            if idx == 0:
                y = y * (HEAD_DIM ** -0.5 * LOG2E)
            o_ref[:, n * nc:(n + 1) * nc] = y.astype(BF16)


def _qkv(x, sc, sh, w, b, tm):
    bsz, l, d = x.shape
    tok = pl.BlockSpec((None, tm, d), lambda i, j: (i, j, 0))
    vec = pl.BlockSpec((None, 1, d), lambda i, j: (i, 0, 0))
    out = jax.ShapeDtypeStruct((bsz, l, d), BF16)
    return pl.pallas_call(
        functools.partial(_qkv_kernel, nc=512),
        grid=(bsz, l // tm),
        in_specs=[tok, vec, vec, _resident((d, 3 * d)), _resident((1, 3 * d))],
        out_specs=[tok, tok, tok],
        out_shape=[out, out, out],
        compiler_params=_params(2),
        name="modulate_qkv",
    )(x, sc, sh, w, b.reshape(1, 3 * d))


def _bias_patterns(rows):
    last = rows - Q_ROWS
    return ((0, 0), (Q_ROWS, Q_ROWS - WIN_H // 2), (last, rows - KEY_ROWS))


def _bias_kernel(t_ref, o_ref, *, rows):
    p = pl.program_id(1)
    for pat, (qrow0, w0) in enumerate(_bias_patterns(rows)):
        @pl.when(p == pat)
        def _():
            for i in range(Q_ROWS):
                r = qrow0 + i
                r0 = min(max(r - WIN_H // 2, 0), rows - WIN_H)
                for j in range(KEY_ROWS):
                    kr = w0 + j
                    lane0 = (j % 2) * GRID_W
                    dst = (slice(i * GRID_W, (i + 1) * GRID_W), slice(j * GRID_W, (j + 1) * GRID_W))
                    if r0 <= kr < r0 + WIN_H:
                        o_ref[dst] = t_ref[kr - r + WIN_H - 1, :, lane0:lane0 + GRID_W] * LOG2E
                    else:
                        o_ref[dst] = jnp.full((GRID_W, GRID_W), NEG_INF, F32)


def _bias_table(rpb, rows):
    nh = rpb.shape[0]
    qc = np.arange(GRID_W)[:, None]
    kc = np.arange(GRID_W)[None, :]
    qstart = np.clip(qc - WIN_W // 2, 0, GRID_W - WIN_W)
    col_ok = (kc >= qstart) & (kc < qstart + WIN_W)
    off = np.clip(kc - qc + WIN_W - 1, 0, 2 * WIN_W - 2)
    toep = jnp.where(jnp.asarray(col_ok), rpb[:, :, off], NEG_INF)
    toep = jnp.concatenate([toep, toep], axis=-1)
    nd = 2 * WIN_H - 1
    return pl.pallas_call(
        functools.partial(_bias_kernel, rows=rows),
        grid=(nh, 3),
        in_specs=[pl.BlockSpec((None, nd, GRID_W, 2 * GRID_W), lambda h, p: (h, 0, 0, 0))],
        out_specs=pl.BlockSpec((None, None, Q_ROWS * GRID_W, KEY_ROWS * GRID_W), lambda h, p: (p, h, 0, 0)),
        out_shape=jax.ShapeDtypeStruct((3, nh, Q_ROWS * GRID_W, KEY_ROWS * GRID_W), F32),
        compiler_params=_params(2),
        name="attn_bias_table",
    )(toep)


def _nattn_kernel(q_ref, k_ref, v_ref, kc_ref, vc_ref, bias_ref, o_ref, *, rows):
    gs = pl.program_id(2)
    n_steps = pl.num_programs(2)
    lanes = q_ref.shape[-1]
    tq = Q_ROWS * GRID_W
    n_keys = KEY_ROWS * GRID_W
    lane = lax.broadcasted_iota(jnp.int32, (1, lanes), 1)
    kc = kc_ref[...]
    vc = vc_ref[...]
    for sg in range(GROUPS_PER_STEP):
        g = gs * GROUPS_PER_STEP + sg
        if sg == 0:
            pat = jnp.where(gs == 0, 0, 1)
        elif sg == GROUPS_PER_STEP - 1:
            pat = jnp.where(gs == n_steps - 1, 2, 1)
        else:
            pat = 1
        w0 = jnp.clip(g * Q_ROWS - WIN_H // 2, 0, rows - KEY_ROWS)
        start = pl.multiple_of(w0 * GRID_W, (WIN_H // 2) * GRID_W)
        q = q_ref[sg * tq:(sg + 1) * tq, :]
        kw = k_ref[pl.ds(start, n_keys), :]
        vall = jnp.concatenate([v_ref[pl.ds(start, n_keys), :], vc], axis=0)
        out = None
        for h in range(HEADS_PER_STEP):
            mine = (lane // HEAD_DIM) == h
            qh = jnp.where(mine, q, jnp.zeros_like(q))
            s = _dot_nt(qh, kw) + bias_ref[pat, h]
            sc = _dot_nt(qh, kc)
            m = jnp.maximum(jnp.max(s, axis=-1, keepdims=True), jnp.max(sc, axis=-1, keepdims=True))
            p = jnp.concatenate([jnp.exp2(s - m).astype(BF16), jnp.exp2(sc - m).astype(BF16)], axis=1)
            o = _dot(p, jnp.where(mine, vall, jnp.ones_like(vall)))
            o = o / pltpu.roll(o, HEAD_DIM, axis=1)
            out = o if out is None else jnp.where(mine, o, out)
        o_ref[sg * tq:(sg + 1) * tq, :] = out.astype(BF16)


def _nattn(q, k, v, kc, vc, bias):
    bsz, n, d = q.shape
    cl = kc.shape[1]
    rows = n // GRID_W
    lanes = HEADS_PER_STEP * HEAD_DIM
    tq = Q_ROWS * GRID_W
    step_rows = GROUPS_PER_STEP * Q_ROWS
    qspec = pl.BlockSpec((None, step_rows * GRID_W, lanes), lambda hp, b, g: (b, g, hp))
    kvspec = pl.BlockSpec((None, n, lanes), lambda hp, b, g: (b, 0, hp))
    cspec = pl.BlockSpec((None, cl, lanes), lambda hp, b, g: (b, 0, hp))
    bspec = pl.BlockSpec((3, HEADS_PER_STEP, tq, KEY_ROWS * GRID_W), lambda hp, b, g: (0, hp, 0, 0),
                         pipeline_mode=pl.Buffered(1))
    return pl.pallas_call(
        functools.partial(_nattn_kernel, rows=rows),
        grid=(d // lanes, bsz, rows // step_rows),
        in_specs=[qspec, kvspec, kvspec, cspec, cspec, bspec],
        out_specs=qspec,
        out_shape=jax.ShapeDtypeStruct((bsz, n, d), BF16),
        compiler_params=_params(3),
        name="neighborhood_attention",
    )(q, k, v, kc, vc, bias)


def _cattn_kernel(q_ref, k_ref, v_ref, o_ref):
    outs = []
    for h in range(HEADS_PER_STEP):
        sl = slice(h * HEAD_DIM, (h + 1) * HEAD_DIM)
        s = _dot_nt(q_ref[:, sl], k_ref[:, sl])
        p = jnp.exp2(s - jnp.max(s, axis=-1, keepdims=True))
        denom = jnp.sum(p, axis=-1, keepdims=True)
        outs.append(_dot(p.astype(BF16), v_ref[:, sl]) / denom)
    o_ref[...] = jnp.concatenate(outs, axis=-1).astype(BF16)


def _cattn(q, k, v):
    bsz, cl, d = q.shape
    lanes = HEADS_PER_STEP * HEAD_DIM
    spec = pl.BlockSpec((None, cl, lanes), lambda b, hp: (b, 0, hp))
    return pl.pallas_call(
        _cattn_kernel,
        grid=(bsz, d // lanes),
        in_specs=[spec, spec, spec],
        out_specs=spec,
        out_shape=jax.ShapeDtypeStruct((bsz, cl, d), BF16),
        compiler_params=_params(2),
        name="context_attention",
    )(q, k, v)


def _proj_ln_kernel(o_ref, w_ref, b_ref, x_ref, g_ref, lg_ref, lb_ref, out_ref, *, alpha):
    y = _dot(o_ref[...], w_ref[...]) + b_ref[...]
    z = alpha * x_ref[...] + g_ref[...] * y
    out_ref[...] = _layer_norm(z, lg_ref[...], lb_ref[...])


def _proj_ln(o, w, b, x, gate, ln_g, ln_b, alpha, tm):
    bsz, l, d = x.shape
    tok = pl.BlockSpec((None, tm, d), lambda i, j: (i, j, 0))
    vec = pl.BlockSpec((None, 1, d), lambda i, j: (i, 0, 0))
    return pl.pallas_call(
        functools.partial(_proj_ln_kernel, alpha=alpha),
        grid=(bsz, l // tm),
        in_specs=[tok, _resident((d, d)), _resident((1, d)), tok, vec, _resident((1, d)), _resident((1, d))],
        out_specs=tok,
        out_shape=jax.ShapeDtypeStruct((bsz, l, d), F32),
        compiler_params=_params(2),
        name="out_proj_deepnorm",
    )(o, w, b.reshape(1, d), x, gate, ln_g.reshape(1, d), ln_b.reshape(1, d))


def _ffn_chunks(d_ff, chunk=512):
    edges = list(range(0, d_ff, chunk)) + [d_ff]
    return list(zip(edges[:-1], edges[1:]))


def _ffn_kernel(x_ref, sc_ref, sh_ref, g_ref, w1_ref, w3_ref, w2_ref, lg_ref, lb_ref, out_ref, *, alpha):
    x = x_ref[...]
    h = _modulate(x, sc_ref[...], sh_ref[...])
    acc = None
    for lo, hi in _ffn_chunks(w1_ref.shape[1]):
        a = _dot(h, w1_ref[:, lo:hi])
        b = _dot(h, w3_ref[:, lo:hi])
        act = (a * jax.nn.sigmoid(a) * b).astype(BF16)
        part = _dot(act, w2_ref[lo:hi, :])
        acc = part if acc is None else acc + part
    z = alpha * x + g_ref[...] * acc
    out_ref[...] = _layer_norm(z, lg_ref[...], lb_ref[...])


def _ffn(x, sc, sh, gate, w1, w3, w2, ln_g, ln_b, alpha, tm):
    bsz, l, d = x.shape
    d_ff = w1.shape[1]
    tok = pl.BlockSpec((None, tm, d), lambda i, j: (i, j, 0))
    vec = pl.BlockSpec((None, 1, d), lambda i, j: (i, 0, 0))
    return pl.pallas_call(
        functools.partial(_ffn_kernel, alpha=alpha),
        grid=(bsz, l // tm),
        in_specs=[tok, vec, vec, vec, _resident((d, d_ff)), _resident((d, d_ff)), _resident((d_ff, d)),
                  _resident((1, d)), _resident((1, d))],
        out_specs=tok,
        out_shape=jax.ShapeDtypeStruct((bsz, l, d), F32),
        compiler_params=_params(2),
        name="swiglu_deepnorm",
    )(x, sc, sh, gate, w1, w3, w2, ln_g.reshape(1, d), ln_b.reshape(1, d))


def _glu_kernel(x_ref, sc_ref, sh_ref, w_ref, b_ref, u_ref, *, nc):
    d = x_ref.shape[-1]
    h = _modulate(x_ref[...], sc_ref[...], sh_ref[...])
    for n in range(d // nc):
        lo = n * nc
        a = _dot(h, w_ref[:, lo:lo + nc]) + b_ref[:, lo:lo + nc]
        g = _dot(h, w_ref[:, d + lo:d + lo + nc]) + b_ref[:, d + lo:d + lo + nc]
        u_ref[:, lo:lo + nc] = a * jax.nn.sigmoid(g)


def _glu(x, sc, sh, w, b, tm):
    bsz, l, d = x.shape
    tok = pl.BlockSpec((None, tm, d), lambda i, j: (i, j, 0))
    vec = pl.BlockSpec((None, 1, d), lambda i, j: (i, 0, 0))
    return pl.pallas_call(
        functools.partial(_glu_kernel, nc=512),
        grid=(bsz, l // tm),
        in_specs=[tok, vec, vec, _resident((d, 2 * d)), _resident((1, 2 * d))],
        out_specs=tok,
        out_shape=jax.ShapeDtypeStruct((bsz, l, d), F32),
        compiler_params=_params(2),
        name="modulate_pw1_glu",
    )(x, sc, sh, w, b.reshape(1, 2 * d))


def _conv_kernel(up_ref, u_ref, un_ref, wdw_ref, bdw_ref, cg_ref, cb_ref, w2_ref, b2_ref,
                 x_ref, g_ref, lg_ref, lb_ref, out_ref, ext_ref, y_ref, *, alpha, rc, lc):
    j = pl.program_id(1)
    tm, d = u_ref.shape
    pad = CONV_K // 2
    ext_ref[0:HALO, :] = jnp.where(j == 0, 0.0, up_ref[...])
    ext_ref[HALO:HALO + tm, :] = u_ref[...]
    ext_ref[HALO + tm:HALO + tm + HALO, :] = jnp.where(j == pl.num_programs(1) - 1, 0.0, un_ref[...])

    span = HALO - pad + CONV_K - 1
    span8 = -(-span // 8) * 8

    def row_chunk(c, carry):
        r0 = pl.multiple_of(c * rc, rc)
        for l0 in range(0, d, lc):
            win = ext_ref[pl.ds(r0, rc + span8), l0:l0 + lc]
            acc = jnp.zeros((rc // 8, 8, lc), F32)
            for r in range(8):
                shifted = win if r == 0 else pltpu.roll(win, rc + span8 - r, axis=0)
                shifted = shifted.reshape((rc + span8) // 8, 8, lc)
                for a in range(span8 // 8):
                    k = 8 * a + r - (HALO - pad)
                    if 0 <= k < CONV_K:
                        acc = acc + shifted[a:a + rc // 8] * wdw_ref[k, :, l0:l0 + lc][None]
            y_ref[pl.ds(r0, rc), l0:l0 + lc] = acc.reshape(rc, lc)
        return carry

    lax.fori_loop(0, tm // rc, row_chunk, 0)
    t = _layer_norm(y_ref[...] + bdw_ref[...], cg_ref[...], cb_ref[...])
    t = (t * jax.nn.sigmoid(t)).astype(BF16)
    y = _dot(t, w2_ref[...]) + b2_ref[...]
    z = alpha * x_ref[...] + g_ref[...] * y
    out_ref[...] = _layer_norm(z, lg_ref[...], lb_ref[...])


def _conv(u, w_dw, b_dw, cg, cb, w2, b2, x, gate, ln_g, ln_b, alpha, tm):
    bsz, l, d = x.shape
    nh = tm // HALO
    last = l // HALO - 1
    tok = pl.BlockSpec((None, tm, d), lambda i, j: (i, j, 0))
    prev = pl.BlockSpec((None, HALO, d), lambda i, j: (i, jnp.maximum(j * nh - 1, 0), 0))
    nxt = pl.BlockSpec((None, HALO, d), lambda i, j: (i, jnp.minimum((j + 1) * nh, last), 0))
    vec = pl.BlockSpec((None, 1, d), lambda i, j: (i, 0, 0))
    row = _resident((1, d))
    return pl.pallas_call(
        functools.partial(_conv_kernel, alpha=alpha, rc=64, lc=128),
        grid=(bsz, l // tm),
        in_specs=[prev, tok, nxt, _resident((CONV_K, 8, d)), row, row, row, _resident((d, d)), row,
                  tok, vec, row, row],
        out_specs=tok,
        out_shape=jax.ShapeDtypeStruct((bsz, l, d), F32),
        scratch_shapes=[pltpu.VMEM((tm + 2 * HALO, d), F32), pltpu.VMEM((tm, d), F32)],
        compiler_params=_params(2),
        name="dwconv_pw2_deepnorm",
    )(u, u, u, jnp.broadcast_to(w_dw[:, None, :], (CONV_K, 8, d)),
      b_dw.reshape(1, d), cg.reshape(1, d), cb.reshape(1, d), w2, b2.reshape(1, d),
      x, gate, ln_g.reshape(1, d), ln_b.reshape(1, d))


def kernel(x, c, ctx, c_ctx, w_ada, b_ada, ln_mix_g, ln_mix_b, ln_ffn_g, ln_ffn_b, attn_w_qkv, attn_b_qkv, attn_w_o, attn_b_o, attn_rpb, conv_w_pw1, conv_b_pw1, conv_w_dw, conv_b_dw, conv_ln_g, conv_ln_b, conv_w_pw2, conv_b_pw2, ffn_w1, ffn_w3, ffn_w2):
    bsz, n, d = x.shape
    cl = ctx.shape[1]
    depth = w_ada.shape[0]
    rows = n // GRID_W
    assert d == N_HEADS * HEAD_DIM and n % (GROUPS_PER_STEP * Q_ROWS * GRID_W) == 0 and rows >= 2 * KEY_ROWS
    alpha = (2 * depth) ** 0.25
    last_attn = max(i for i in range(depth) if i % N_MIXERS == 0)
    tm = min(512, n)
    tmc = min(512, cl)

    s_rows = jnp.zeros((16, d), F32).at[:bsz].set(c).at[bsz].set(c_ctx)
    ada = _ada_all(s_rows, w_ada, b_ada)

    def mods(i):
        lat = [ada[i, :bsz, k * d:(k + 1) * d].reshape(bsz, 1, d) for k in range(6)]
        cx = [jnp.broadcast_to(ada[i, bsz, k * d:(k + 1) * d].reshape(1, 1, d), (bsz, 1, d)) for k in range(6)]
        return lat, cx

    bf = lambda w: w.astype(BF16)
    xc = ctx
    for i in range(depth):
        slot = i // N_MIXERS
        ctx_in = i <= last_attn
        ctx_live = i < last_attn
        (sh1, sc1, g1, sh2, sc2, g2), (csh1, csc1, cg1, csh2, csc2, cg2) = mods(i)
        w1, w3, w2 = bf(ffn_w1[i]), bf(ffn_w3[i]), bf(ffn_w2[i])
        if i % N_MIXERS == 0:
            wqkv, wo = bf(attn_w_qkv[slot]), bf(attn_w_o[slot])
            q, k, v = _qkv(x, sc1, sh1, wqkv, attn_b_qkv[slot], tm)
            qc, kc, vc = _qkv(xc, csc1, csh1, wqkv, attn_b_qkv[slot], tmc)
            bias = _bias_table(attn_rpb[slot], rows)
            o = _nattn(q, k, v, kc, vc, bias)
            x = _proj_ln(o, wo, attn_b_o[slot], x, g1, ln_mix_g[i], ln_mix_b[i], alpha, tm)
            if ctx_live:
                oc = _cattn(qc, kc, vc)
                xc = _proj_ln(oc, wo, attn_b_o[slot], xc, cg1, ln_mix_g[i], ln_mix_b[i], alpha, tmc)
        else:
            wp1, wp2 = bf(conv_w_pw1[slot]), bf(conv_w_pw2[slot])
            cargs = (conv_w_dw[slot], conv_b_dw[slot], conv_ln_g[slot], conv_ln_b[slot], wp2, conv_b_pw2[slot])
            u = _glu(x, sc1, sh1, wp1, conv_b_pw1[slot], tm)
            x = _conv(u, *cargs, x, g1, ln_mix_g[i], ln_mix_b[i], alpha, tm)
            if ctx_live:
                uc = _glu(xc, csc1, csh1, wp1, conv_b_pw1[slot], tmc)
                xc = _conv(uc, *cargs, xc, cg1, ln_mix_g[i], ln_mix_b[i], alpha, tmc)
        x = _ffn(x, sc2, sh2, g2, w1, w3, w2, ln_ffn_g[i], ln_ffn_b[i], alpha, tm)
        if ctx_live:
            xc = _ffn(xc, csc2, csh2, cg2, w1, w3, w2, ln_ffn_g[i], ln_ffn_b[i], alpha, tmc)
    return x
```

```python
import functools

import numpy as np
import jax
import jax.numpy as jnp
from jax import lax
from jax.experimental import pallas as pl
from jax.experimental.pallas import tpu as pltpu

F32 = jnp.float32
BF16 = jnp.bfloat16

N_HEADS = 16
HEAD_DIM = 64
GRID_W = 64
WIN_H = 8
WIN_W = 16
CONV_K = 31
N_MIXERS = 2
LN_EPS = 1e-5
NEG_INF = -1e30

LOG2E = 1.4426950408889634
Q_ROWS = 4
KEY_ROWS = Q_ROWS + WIN_H
GROUPS_PER_STEP = 8
HEADS_PER_STEP = 2
HALO = 16
VMEM_LIMIT = 56 * 1024 * 1024


def _params(n_axes):
    return pltpu.CompilerParams(dimension_semantics=("arbitrary",) * n_axes,
                                vmem_limit_bytes=VMEM_LIMIT)


def _resident(shape):
    return pl.BlockSpec(shape, lambda *_: (0,) * len(shape), pipeline_mode=pl.Buffered(1))


def _layer_norm(z, g, b):
    mu = jnp.mean(z, axis=-1, keepdims=True)
    zc = z - mu
    var = jnp.mean(zc * zc, axis=-1, keepdims=True)
    return zc * lax.rsqrt(var + LN_EPS) * g + b


def _modulate(x, sc, sh):
    return (x * (1.0 + sc) + sh).astype(BF16)


def _dot(a, b):
    return jnp.dot(a, b, preferred_element_type=F32)


def _dot_nt(a, b):
    return lax.dot_general(a, b, (((1,), (1,)), ((), ())), preferred_element_type=F32)


def _ada_kernel(s_ref, w_ref, b_ref, o_ref):
    s = s_ref[...]
    s = s * jax.nn.sigmoid(s)
    o_ref[...] = _dot(s.astype(BF16), w_ref[...].astype(BF16)) + b_ref[...]


def _ada_all(s_rows, w_ada, b_ada, tn=1536):
    depth, d, n = w_ada.shape
    rows = s_rows.shape[0]
    return pl.pallas_call(
        _ada_kernel,
        grid=(depth, n // tn),
        in_specs=[pl.BlockSpec((rows, d), lambda i, j: (0, 0)),
                  pl.BlockSpec((None, d, tn), lambda i, j: (i, 0, j)),
                  pl.BlockSpec((None, 1, tn), lambda i, j: (i, 0, j))],
        out_specs=pl.BlockSpec((None, rows, tn), lambda i, j: (i, 0, j)),
        out_shape=jax.ShapeDtypeStruct((depth, rows, n), F32),
        compiler_params=_params(2),
        name="ada_params",
    )(s_rows, w_ada, b_ada.reshape(depth, 1, n))


def _qkv_kernel(x_ref, sc_ref, sh_ref, w_ref, b_ref, q_ref, k_ref, v_ref, *, nc):
    d = x_ref.shape[-1]
    h = _modulate(x_ref[...], sc_ref[...], sh_ref[...])
    for idx, o_ref in enumerate((q_ref, k_ref, v_ref)):
        for n in range(d // nc):
            col = idx * d + n * nc
            y = _dot(h, w_ref[:, col:col + nc]) + b_ref[:, col:col + nc]
            if idx == 0:
                y = y * (HEAD_DIM ** -0.5 * LOG2E)
            o_ref[:, n * nc:(n + 1) * nc] = y.astype(BF16)


def _qkv(x, sc, sh, w, b, tm):
    bsz, l, d = x.shape
    tok = pl.BlockSpec((None, tm, d), lambda i, j: (i, j, 0))
    vec = pl.BlockSpec((None, 1, d), lambda i, j: (i, 0, 0))
    out = jax.ShapeDtypeStruct((bsz, l, d), BF16)
    return pl.pallas_call(
        functools.partial(_qkv_kernel, nc=512),
        grid=(bsz, l // tm),
        in_specs=[tok, vec, vec, _resident((d, 3 * d)), _resident((1, 3 * d))],
        out_specs=[tok, tok, tok],
        out_shape=[out, out, out],
        compiler_params=_params(2),
        name="modulate_qkv",
    )(x, sc, sh, w, b.reshape(1, 3 * d))


def _bias_patterns(rows):
    last = rows - Q_ROWS
    return ((0, 0), (Q_ROWS, Q_ROWS - WIN_H // 2), (last, rows - KEY_ROWS))


def _bias_kernel(t_ref, o_ref, *, rows):
    p = pl.program_id(1)
    for pat, (qrow0, w0) in enumerate(_bias_patterns(rows)):
        @pl.when(p == pat)
        def _():
            for i in range(Q_ROWS):
                r = qrow0 + i
                r0 = min(max(r - WIN_H // 2, 0), rows - WIN_H)
                for j in range(KEY_ROWS):
                    kr = w0 + j
                    lane0 = (j % 2) * GRID_W
                    dst = (slice(i * GRID_W, (i + 1) * GRID_W), slice(j * GRID_W, (j + 1) * GRID_W))
                    if r0 <= kr < r0 + WIN_H:
                        o_ref[dst] = t_ref[kr - r + WIN_H - 1, :, lane0:lane0 + GRID_W] * LOG2E
                    else:
                        o_ref[dst] = jnp.full((GRID_W, GRID_W), NEG_INF, F32)


def _bias_table(rpb, rows):
    nh = rpb.shape[0]
    qc = np.arange(GRID_W)[:, None]
    kc = np.arange(GRID_W)[None, :]
    qstart = np.clip(qc - WIN_W // 2, 0, GRID_W - WIN_W)
    col_ok = (kc >= qstart) & (kc < qstart + WIN_W)
    off = np.clip(kc - qc + WIN_W - 1, 0, 2 * WIN_W - 2)
    toep = jnp.where(jnp.asarray(col_ok), rpb[:, :, off], NEG_INF)
    toep = jnp.concatenate([toep, toep], axis=-1)
    nd = 2 * WIN_H - 1
    return pl.pallas_call(
        functools.partial(_bias_kernel, rows=rows),
        grid=(nh, 3),
        in_specs=[pl.BlockSpec((None, nd, GRID_W, 2 * GRID_W), lambda h, p: (h, 0, 0, 0))],
        out_specs=pl.BlockSpec((None, None, Q_ROWS * GRID_W, KEY_ROWS * GRID_W), lambda h, p: (p, h, 0, 0)),
        out_shape=jax.ShapeDtypeStruct((3, nh, Q_ROWS * GRID_W, KEY_ROWS * GRID_W), F32),
        compiler_params=_params(2),
        name="attn_bias_table",
    )(toep)


def _nattn_kernel(q_ref, k_ref, v_ref, kc_ref, vc_ref, bias_ref, o_ref, *, rows):
    gs = pl.program_id(2)
    n_steps = pl.num_programs(2)
    lanes = q_ref.shape[-1]
    tq = Q_ROWS * GRID_W
    n_keys = KEY_ROWS * GRID_W
    lane = lax.broadcasted_iota(jnp.int32, (1, lanes), 1)
    kc = kc_ref[...]
    vc = vc_ref[...]
    for sg in range(GROUPS_PER_STEP):
        g = gs * GROUPS_PER_STEP + sg
        if sg == 0:
            pat = jnp.where(gs == 0, 0, 1)
        elif sg == GROUPS_PER_STEP - 1:
            pat = jnp.where(gs == n_steps - 1, 2, 1)
        else:
            pat = 1
        w0 = jnp.clip(g * Q_ROWS - WIN_H // 2, 0, rows - KEY_ROWS)
        start = pl.multiple_of(w0 * GRID_W, (WIN_H // 2) * GRID_W)
        q = q_ref[sg * tq:(sg + 1) * tq, :]
        kw = k_ref[pl.ds(start, n_keys), :]
        vall = jnp.concatenate([v_ref[pl.ds(start, n_keys), :], vc], axis=0)
        vext = jnp.concatenate([vall, jnp.ones_like(vall)], axis=1)
        qq = jnp.concatenate([jnp.where((lane // HEAD_DIM) == h, q, jnp.zeros_like(q))
                              for h in range(HEADS_PER_STEP)], axis=0)
        s = _dot_nt(qq, kw) + bias_ref[pat].reshape(HEADS_PER_STEP * tq, n_keys)
        sc = _dot_nt(qq, kc)
        m = jnp.maximum(jnp.max(s, axis=-1, keepdims=True), jnp.max(sc, axis=-1, keepdims=True))
        p = jnp.concatenate([jnp.exp2(s - m).astype(BF16), jnp.exp2(sc - m).astype(BF16)], axis=1)
        o = _dot(p, vext)
        o = o[:, :lanes] / o[:, lanes:]
        o_ref[sg * tq:(sg + 1) * tq, :] = jnp.where(lane < HEAD_DIM, o[:tq], o[tq:]).astype(BF16)


def _nattn(q, k, v, kc, vc, bias):
    bsz, n, d = q.shape
    cl = kc.shape[1]
    rows = n // GRID_W
    lanes = HEADS_PER_STEP * HEAD_DIM
    tq = Q_ROWS * GRID_W
    step_rows = GROUPS_PER_STEP * Q_ROWS
    qspec = pl.BlockSpec((None, step_rows * GRID_W, lanes), lambda hp, b, g: (b, g, hp))
    kvspec = pl.BlockSpec((None, n, lanes), lambda hp, b, g: (b, 0, hp))
    cspec = pl.BlockSpec((None, cl, lanes), lambda hp, b, g: (b, 0, hp))
    bspec = pl.BlockSpec((3, HEADS_PER_STEP, tq, KEY_ROWS * GRID_W), lambda hp, b, g: (0, hp, 0, 0),
                         pipeline_mode=pl.Buffered(1))
    return pl.pallas_call(
        functools.partial(_nattn_kernel, rows=rows),
        grid=(d // lanes, bsz, rows // step_rows),
        in_specs=[qspec, kvspec, kvspec, cspec, cspec, bspec],
        out_specs=qspec,
        out_shape=jax.ShapeDtypeStruct((bsz, n, d), BF16),
        compiler_params=_params(3),
        name="neighborhood_attention",
    )(q, k, v, kc, vc, bias)


def _cattn_kernel(q_ref, k_ref, v_ref, o_ref):
    outs = []
    for h in range(HEADS_PER_STEP):
        sl = slice(h * HEAD_DIM, (h + 1) * HEAD_DIM)
        s = _dot_nt(q_ref[:, sl], k_ref[:, sl])
        p = jnp.exp2(s - jnp.max(s, axis=-1, keepdims=True))
        denom = jnp.sum(p, axis=-1, keepdims=True)
        outs.append(_dot(p.astype(BF16), v_ref[:, sl]) / denom)
    o_ref[...] = jnp.concatenate(outs, axis=-1).astype(BF16)


def _cattn(q, k, v):
    bsz, cl, d = q.shape
    lanes = HEADS_PER_STEP * HEAD_DIM
    spec = pl.BlockSpec((None, cl, lanes), lambda b, hp: (b, 0, hp))
    return pl.pallas_call(
        _cattn_kernel,
        grid=(bsz, d // lanes),
        in_specs=[spec, spec, spec],
        out_specs=spec,
        out_shape=jax.ShapeDtypeStruct((bsz, cl, d), BF16),
        compiler_params=_params(2),
        name="context_attention",
    )(q, k, v)


def _proj_ln_kernel(o_ref, w_ref, b_ref, x_ref, g_ref, lg_ref, lb_ref, out_ref, *, alpha):
    y = _dot(o_ref[...], w_ref[...]) + b_ref[...]
    z = alpha * x_ref[...] + g_ref[...] * y
    out_ref[...] = _layer_norm(z, lg_ref[...], lb_ref[...])


def _proj_ln(o, w, b, x, gate, ln_g, ln_b, alpha, tm):
    bsz, l, d = x.shape
    tok = pl.BlockSpec((None, tm, d), lambda i, j: (i, j, 0))
    vec = pl.BlockSpec((None, 1, d), lambda i, j: (i, 0, 0))
    return pl.pallas_call(
        functools.partial(_proj_ln_kernel, alpha=alpha),
        grid=(bsz, l // tm),
        in_specs=[tok, _resident((d, d)), _resident((1, d)), tok, vec, _resident((1, d)), _resident((1, d))],
        out_specs=tok,
        out_shape=jax.ShapeDtypeStruct((bsz, l, d), F32),
        compiler_params=_params(2),
        name="out_proj_deepnorm",
    )(o, w, b.reshape(1, d), x, gate, ln_g.reshape(1, d), ln_b.reshape(1, d))


def _ffn_chunks(d_ff, chunk=512):
    edges = list(range(0, d_ff, chunk)) + [d_ff]
    return list(zip(edges[:-1], edges[1:]))


def _ffn_kernel(x_ref, sc_ref, sh_ref, g_ref, w1_ref, w3_ref, w2_ref, lg_ref, lb_ref, out_ref, *, alpha):
    x = x_ref[...]
    h = _modulate(x, sc_ref[...], sh_ref[...])
    acc = None
    for lo, hi in _ffn_chunks(w1_ref.shape[1]):
        a = _dot(h, w1_ref[:, lo:hi])
        b = _dot(h, w3_ref[:, lo:hi])
        act = (a * jax.nn.sigmoid(a) * b).astype(BF16)
        part = _dot(act, w2_ref[lo:hi, :])
        acc = part if acc is None else acc + part
    z = alpha * x + g_ref[...] * acc
    out_ref[...] = _layer_norm(z, lg_ref[...], lb_ref[...])


def _ffn(x, sc, sh, gate, w1, w3, w2, ln_g, ln_b, alpha, tm):
    bsz, l, d = x.shape
    d_ff = w1.shape[1]
    tok = pl.BlockSpec((None, tm, d), lambda i, j: (i, j, 0))
    vec = pl.BlockSpec((None, 1, d), lambda i, j: (i, 0, 0))
    return pl.pallas_call(
        functools.partial(_ffn_kernel, alpha=alpha),
        grid=(bsz, l // tm),
        in_specs=[tok, vec, vec, vec, _resident((d, d_ff)), _resident((d, d_ff)), _resident((d_ff, d)),
                  _resident((1, d)), _resident((1, d))],
        out_specs=tok,
        out_shape=jax.ShapeDtypeStruct((bsz, l, d), F32),
        compiler_params=_params(2),
        name="swiglu_deepnorm",
    )(x, sc, sh, gate, w1, w3, w2, ln_g.reshape(1, d), ln_b.reshape(1, d))


def _glu_kernel(x_ref, sc_ref, sh_ref, w_ref, b_ref, u_ref, *, nc):
    d = x_ref.shape[-1]
    h = _modulate(x_ref[...], sc_ref[...], sh_ref[...])
    for n in range(d // nc):
        lo = n * nc
        a = _dot(h, w_ref[:, lo:lo + nc]) + b_ref[:, lo:lo + nc]
        g = _dot(h, w_ref[:, d + lo:d + lo + nc]) + b_ref[:, d + lo:d + lo + nc]
        u_ref[:, lo:lo + nc] = a * jax.nn.sigmoid(g)


def _glu(x, sc, sh, w, b, tm):
    bsz, l, d = x.shape
    tok = pl.BlockSpec((None, tm, d), lambda i, j: (i, j, 0))
    vec = pl.BlockSpec((None, 1, d), lambda i, j: (i, 0, 0))
    return pl.pallas_call(
        functools.partial(_glu_kernel, nc=512),
        grid=(bsz, l // tm),
        in_specs=[tok, vec, vec, _resident((d, 2 * d)), _resident((1, 2 * d))],
        out_specs=tok,
        out_shape=jax.ShapeDtypeStruct((bsz, l, d), F32),
        compiler_params=_params(2),
        name="modulate_pw1_glu",
    )(x, sc, sh, w, b.reshape(1, 2 * d))


def _conv_kernel(up_ref, u_ref, un_ref, wdw_ref, bdw_ref, cg_ref, cb_ref, w2_ref, b2_ref,
                 x_ref, g_ref, lg_ref, lb_ref, out_ref, ext_ref, y_ref, *, alpha, rc, lc):
    j = pl.program_id(1)
    tm, d = u_ref.shape
    pad = CONV_K // 2
    ext_ref[0:HALO, :] = jnp.where(j == 0, 0.0, up_ref[...])
    ext_ref[HALO:HALO + tm, :] = u_ref[...]
    ext_ref[HALO + tm:HALO + tm + HALO, :] = jnp.where(j == pl.num_programs(1) - 1, 0.0, un_ref[...])

    span = HALO - pad + CONV_K - 1
    span8 = -(-span // 8) * 8

    def row_chunk(c, carry):
        r0 = pl.multiple_of(c * rc, rc)
        for l0 in range(0, d, lc):
            win = ext_ref[pl.ds(r0, rc + span8), l0:l0 + lc]
            acc = jnp.zeros((rc // 8, 8, lc), F32)
            for r in range(8):
                shifted = win if r == 0 else pltpu.roll(win, rc + span8 - r, axis=0)
                shifted = shifted.reshape((rc + span8) // 8, 8, lc)
                for a in range(span8 // 8):
                    k = 8 * a + r - (HALO - pad)
                    if 0 <= k < CONV_K:
                        acc = acc + shifted[a:a + rc // 8] * wdw_ref[k, :, l0:l0 + lc][None]
            y_ref[pl.ds(r0, rc), l0:l0 + lc] = acc.reshape(rc, lc)
        return carry

    lax.fori_loop(0, tm // rc, row_chunk, 0)
    t = _layer_norm(y_ref[...] + bdw_ref[...], cg_ref[...], cb_ref[...])
    t = (t * jax.nn.sigmoid(t)).astype(BF16)
    y = _dot(t, w2_ref[...]) + b2_ref[...]
    z = alpha * x_ref[...] + g_ref[...] * y
    out_ref[...] = _layer_norm(z, lg_ref[...], lb_ref[...])


def _conv(u, w_dw, b_dw, cg, cb, w2, b2, x, gate, ln_g, ln_b, alpha, tm):
    bsz, l, d = x.shape
    nh = tm // HALO
    last = l // HALO - 1
    tok = pl.BlockSpec((None, tm, d), lambda i, j: (i, j, 0))
    prev = pl.BlockSpec((None, HALO, d), lambda i, j: (i, jnp.maximum(j * nh - 1, 0), 0))
    nxt = pl.BlockSpec((None, HALO, d), lambda i, j: (i, jnp.minimum((j + 1) * nh, last), 0))
    vec = pl.BlockSpec((None, 1, d), lambda i, j: (i, 0, 0))
    row = _resident((1, d))
    return pl.pallas_call(
        functools.partial(_conv_kernel, alpha=alpha, rc=64, lc=128),
        grid=(bsz, l // tm),
        in_specs=[prev, tok, nxt, _resident((CONV_K, 8, d)), row, row, row, _resident((d, d)), row,
                  tok, vec, row, row],
        out_specs=tok,
        out_shape=jax.ShapeDtypeStruct((bsz, l, d), F32),
        scratch_shapes=[pltpu.VMEM((tm + 2 * HALO, d), F32), pltpu.VMEM((tm, d), F32)],
        compiler_params=_params(2),
        name="dwconv_pw2_deepnorm",
    )(u, u, u, jnp.broadcast_to(w_dw[:, None, :], (CONV_K, 8, d)),
      b_dw.reshape(1, d), cg.reshape(1, d), cb.reshape(1, d), w2, b2.reshape(1, d),
      x, gate, ln_g.reshape(1, d), ln_b.reshape(1, d))


def _conv_ffn_kernel(up_ref, u_ref, un_ref, wdw_ref, bdw_ref, cg_ref, cb_ref, wp2_ref, bp2_ref,
                     x_ref, g1_ref, lmg_ref, lmb_ref,
                     sc2_ref, sh2_ref, g2_ref, w1_ref, w3_ref, w2_ref, lfg_ref, lfb_ref,
                     out_ref, ext_ref, y_ref, x1_ref, h_ref, acc_ref, *, alpha, total, n_tiles, rc, lc, fc):
    i = pl.program_id(0)
    j = lax.rem(jnp.minimum(i, total - 1), n_tiles)
    tm, d = u_ref.shape
    d_ff = w1_ref.shape[1]
    pad = CONV_K // 2
    span8 = -(-(HALO - pad + CONV_K - 1) // 8) * 8

    @pl.when(i == 0)
    def _():
        x1_ref[...] = jnp.zeros_like(x1_ref)

    ext_ref[0:HALO, :] = jnp.where(j == 0, 0.0, up_ref[...])
    ext_ref[HALO:HALO + tm, :] = u_ref[...]
    ext_ref[HALO + tm:HALO + tm + HALO, :] = jnp.where(j == n_tiles - 1, 0.0, un_ref[...])
    h_ref[...] = _modulate(x1_ref[...], sc2_ref[...], sh2_ref[...])
    acc_ref[...] = jnp.zeros_like(acc_ref)

    lane_chunks = d // lc
    n_units = (tm // rc) * lane_chunks
    n_fc = d_ff // fc
    units_per_iter = -(-n_units // n_fc)

    def conv_unit(unit):
        r0 = pl.multiple_of((unit // lane_chunks) * rc, rc)
        l0 = pl.multiple_of(lax.rem(unit, lane_chunks) * lc, lc)
        win = ext_ref[pl.ds(r0, rc + span8), pl.ds(l0, lc)]
        acc = jnp.zeros((rc // 8, 8, lc), F32)
        for r in range(8):
            shifted = win if r == 0 else pltpu.roll(win, rc + span8 - r, axis=0)
            shifted = shifted.reshape((rc + span8) // 8, 8, lc)
            for a in range(span8 // 8):
                k = 8 * a + r - (HALO - pad)
                if 0 <= k < CONV_K:
                    acc = acc + shifted[a:a + rc // 8] * wdw_ref[k, :, pl.ds(l0, lc)][None]
        y_ref[pl.ds(r0, rc), pl.ds(l0, lc)] = acc.reshape(rc, lc)

    def body(c, carry):
        f0 = pl.multiple_of(c * fc, fc)
        h = h_ref[...]
        a = _dot(h, w1_ref[:, pl.ds(f0, fc)])
        b = _dot(h, w3_ref[:, pl.ds(f0, fc)])
        act = (a * jax.nn.sigmoid(a) * b).astype(BF16)
        acc_ref[...] += _dot(act, w2_ref[pl.ds(f0, fc), :])
        for uu in range(units_per_iter):
            conv_unit(jnp.minimum(c * units_per_iter + uu, n_units - 1))
        return carry

    lax.fori_loop(0, n_fc, body, 0)

    z = alpha * x1_ref[...] + g2_ref[...] * acc_ref[...]
    out_ref[...] = _layer_norm(z, lfg_ref[...], lfb_ref[...])

    t = _layer_norm(y_ref[...] + bdw_ref[...], cg_ref[...], cb_ref[...])
    t = (t * jax.nn.sigmoid(t)).astype(BF16)
    y = _dot(t, wp2_ref[...]) + bp2_ref[...]
    z1 = alpha * x_ref[...] + g1_ref[...] * y
    x1_ref[...] = _layer_norm(z1, lmg_ref[...], lmb_ref[...])


def _conv_ffn(u, w_dw, b_dw, cg, cb, wp2, bp2, x, g1, lmg, lmb, sc2, sh2, g2, w1, w3, w2, lfg, lfb, alpha, tm):
    bsz, l, d = x.shape
    d_ff = w1.shape[1]
    n_tiles = l // tm
    total = bsz * n_tiles
    nh = tm // HALO
    last = l // HALO - 1
    fc = 256
    assert d_ff % fc == 0

    def cur(i):
        t = jnp.minimum(i, total - 1)
        return t // n_tiles, lax.rem(t, n_tiles)

    def prv(i):
        t = jnp.maximum(i - 1, 0)
        return t // n_tiles, lax.rem(t, n_tiles)

    tok_cur = pl.BlockSpec((None, tm, d), lambda i: (*cur(i), 0))
    tok_prv = pl.BlockSpec((None, tm, d), lambda i: (*prv(i), 0))
    halo_prev = pl.BlockSpec((None, HALO, d), lambda i: (cur(i)[0], jnp.maximum(cur(i)[1] * nh - 1, 0), 0))
    halo_next = pl.BlockSpec((None, HALO, d), lambda i: (cur(i)[0], jnp.minimum((cur(i)[1] + 1) * nh, last), 0))
    vec_cur = pl.BlockSpec((None, 1, d), lambda i: (cur(i)[0], 0, 0))
    vec_prv = pl.BlockSpec((None, 1, d), lambda i: (prv(i)[0], 0, 0))
    row = _resident((1, d))
    r2 = lambda a: a.reshape(1, d)
    return pl.pallas_call(
        functools.partial(_conv_ffn_kernel, alpha=alpha, total=total, n_tiles=n_tiles, rc=64, lc=128, fc=fc),
        grid=(total + 1,),
        in_specs=[halo_prev, tok_cur, halo_next, _resident((CONV_K, 8, d)), row, row, row, _resident((d, d)), row,
                  tok_cur, vec_cur, row, row,
                  vec_prv, vec_prv, vec_prv, _resident((d, d_ff)), _resident((d, d_ff)), _resident((d_ff, d)),
                  row, row],
        out_specs=tok_prv,
        out_shape=jax.ShapeDtypeStruct((bsz, l, d), F32),
        scratch_shapes=[pltpu.VMEM((tm + 2 * HALO, d), F32), pltpu.VMEM((tm, d), F32), pltpu.VMEM((tm, d), F32),
                        pltpu.VMEM((tm, d), BF16), pltpu.VMEM((tm, d), F32)],
        compiler_params=_params(1),
        name="dwconv_pw2_swiglu_pipelined",
    )(u, u, u, jnp.broadcast_to(w_dw[:, None, :], (CONV_K, 8, d)), r2(b_dw), r2(cg), r2(cb), wp2, r2(bp2),
      x, g1, r2(lmg), r2(lmb), sc2, sh2, g2, w1, w3, w2, r2(lfg), r2(lfb))


def kernel(x, c, ctx, c_ctx, w_ada, b_ada, ln_mix_g, ln_mix_b, ln_ffn_g, ln_ffn_b, attn_w_qkv, attn_b_qkv, attn_w_o, attn_b_o, attn_rpb, conv_w_pw1, conv_b_pw1, conv_w_dw, conv_b_dw, conv_ln_g, conv_ln_b, conv_w_pw2, conv_b_pw2, ffn_w1, ffn_w3, ffn_w2):
    bsz, n, d = x.shape
    cl = ctx.shape[1]
    depth = w_ada.shape[0]
    rows = n // GRID_W
    assert d == N_HEADS * HEAD_DIM and n % (GROUPS_PER_STEP * Q_ROWS * GRID_W) == 0 and rows >= 2 * KEY_ROWS
    alpha = (2 * depth) ** 0.25
    last_attn = max(i for i in range(depth) if i % N_MIXERS == 0)
    tm = min(512, n)
    tmc = min(512, cl)

    s_rows = jnp.zeros((16, d), F32).at[:bsz].set(c).at[bsz].set(c_ctx)
    ada = _ada_all(s_rows, w_ada, b_ada)

    def mods(i):
        lat = [ada[i, :bsz, k * d:(k + 1) * d].reshape(bsz, 1, d) for k in range(6)]
        cx = [jnp.broadcast_to(ada[i, bsz, k * d:(k + 1) * d].reshape(1, 1, d), (bsz, 1, d)) for k in range(6)]
        return lat, cx

    bf = lambda w: w.astype(BF16)
    xc = ctx
    for i in range(depth):
        slot = i // N_MIXERS
        ctx_in = i <= last_attn
        ctx_live = i < last_attn
        (sh1, sc1, g1, sh2, sc2, g2), (csh1, csc1, cg1, csh2, csc2, cg2) = mods(i)
        w1, w3, w2 = bf(ffn_w1[i]), bf(ffn_w3[i]), bf(ffn_w2[i])
        if i % N_MIXERS == 0:
            wqkv, wo = bf(attn_w_qkv[slot]), bf(attn_w_o[slot])
            q, k, v = _qkv(x, sc1, sh1, wqkv, attn_b_qkv[slot], tm)
            qc, kc, vc = _qkv(xc, csc1, csh1, wqkv, attn_b_qkv[slot], tmc)
            bias = _bias_table(attn_rpb[slot], rows)
            o = _nattn(q, k, v, kc, vc, bias)
            x = _proj_ln(o, wo, attn_b_o[slot], x, g1, ln_mix_g[i], ln_mix_b[i], alpha, tm)
            if ctx_live:
                oc = _cattn(qc, kc, vc)
                xc = _proj_ln(oc, wo, attn_b_o[slot], xc, cg1, ln_mix_g[i], ln_mix_b[i], alpha, tmc)
        else:
            wp1, wp2 = bf(conv_w_pw1[slot]), bf(conv_w_pw2[slot])
            cargs = (conv_w_dw[slot], conv_b_dw[slot], conv_ln_g[slot], conv_ln_b[slot], wp2, conv_b_pw2[slot])
            u = _glu(x, sc1, sh1, wp1, conv_b_pw1[slot], tm)
            x = _conv_ffn(u, *cargs, x, g1, ln_mix_g[i], ln_mix_b[i], sc2, sh2, g2, w1, w3, w2,
                          ln_ffn_g[i], ln_ffn_b[i], alpha, tm)
            if ctx_live:
                uc = _glu(xc, csc1, csh1, wp1, conv_b_pw1[slot], tmc)
                xc = _conv(uc, *cargs, xc, cg1, ln_mix_g[i], ln_mix_b[i], alpha, tmc)
        if i % N_MIXERS == 0:
            x = _ffn(x, sc2, sh2, g2, w1, w3, w2, ln_ffn_g[i], ln_ffn_b[i], alpha, tm)
        if ctx_live:
            xc = _ffn(xc, csc2, csh2, cg2, w1, w3, w2, ln_ffn_g[i], ln_ffn_b[i], alpha, tmc)
    return x
```

```python
import functools

import numpy as np
import jax
import jax.numpy as jnp
from jax import lax
from jax.experimental import pallas as pl
from jax.experimental.pallas import tpu as pltpu

F32 = jnp.float32
BF16 = jnp.bfloat16

N_HEADS = 16
HEAD_DIM = 64
GRID_W = 64
WIN_H = 8
WIN_W = 16
CONV_K = 31
N_MIXERS = 2
N_MODS = 6
LN_EPS = 1e-5
NEG_INF = -1e30

LOG2E = 1.4426950408889634
Q_ROWS = 4
KEY_ROWS = Q_ROWS + WIN_H
GROUPS_PER_STEP = 16
HEADS_PER_STEP = 2
HALO = 16
ADA_ROWS = 16
VMEM_LIMIT = 56 * 1024 * 1024


def _params(n_axes):
    return pltpu.CompilerParams(dimension_semantics=("arbitrary",) * n_axes,
                                vmem_limit_bytes=VMEM_LIMIT)


def _layer_block(layer, shape):
    return pl.BlockSpec((None,) + tuple(shape), lambda *_: (layer,) + (0,) * len(shape),
                        pipeline_mode=pl.Buffered(1))


def _mod_block(layer, chunk, d, ctx_row):
    if ctx_row is None:
        return pl.BlockSpec((None, None, 1, d), lambda i, j: (layer, i, 0, chunk))
    return pl.BlockSpec((None, None, 1, d), lambda i, j: (layer, ctx_row, 0, chunk))


def _layer_norm(z, g, b):
    mu = jnp.mean(z, axis=-1, keepdims=True)
    zc = z - mu
    var = jnp.mean(zc * zc, axis=-1, keepdims=True)
    return zc * lax.rsqrt(var + LN_EPS) * g + b


def _modulate(x, sc, sh):
    return (x * (1.0 + sc) + sh).astype(BF16)


def _dot(a, b):
    return jnp.dot(a, b, preferred_element_type=F32)


def _dot_nt(a, b):
    return lax.dot_general(a, b, (((1,), (1,)), ((), ())), preferred_element_type=F32)


def _ada_kernel(s_ref, w_ref, b_ref, o_ref):
    s = s_ref[...]
    s = s * jax.nn.sigmoid(s)
    o_ref[...] = _dot(s.astype(BF16), w_ref[...].astype(BF16)) + b_ref[...]


def _ada_all(s_rows, w_ada, b_ada, tn=1536):
    depth, d, n = w_ada.shape
    rows = s_rows.shape[0]
    return pl.pallas_call(
        _ada_kernel,
        grid=(depth, n // tn),
        in_specs=[pl.BlockSpec((rows, d), lambda i, j: (0, 0)),
                  pl.BlockSpec((None, d, tn), lambda i, j: (i, 0, j)),
                  pl.BlockSpec((None, 1, tn), lambda i, j: (i, 0, j))],
        out_specs=pl.BlockSpec((None, rows, tn), lambda i, j: (i, 0, j)),
        out_shape=jax.ShapeDtypeStruct((depth, rows, n), F32),
        compiler_params=_params(2),
        name="ada_params",
    )(s_rows, w_ada, b_ada.reshape(depth, 1, n))


def _qkv_kernel(x_ref, sc_ref, sh_ref, w_ref, b_ref, q_ref, k_ref, v_ref, *, nc):
    d = x_ref.shape[-1]
    h = _modulate(x_ref[...], sc_ref[...], sh_ref[...])
    for idx, o_ref in enumerate((q_ref, k_ref, v_ref)):
        for n in range(d // nc):
            col = idx * d + n * nc
            y = _dot(h, w_ref[:, col:col + nc]) + b_ref[:, col:col + nc]
            if idx == 0:
                y = y * (HEAD_DIM ** -0.5 * LOG2E)
            o_ref[:, n * nc:(n + 1) * nc] = y.astype(BF16)


def _qkv(x, ada, layer, ctx_row, w, b, slot, tm):
    bsz, l, d = x.shape
    tok = pl.BlockSpec((None, tm, d), lambda i, j: (i, j, 0))
    out = jax.ShapeDtypeStruct((bsz, l, d), BF16)
    return pl.pallas_call(
        functools.partial(_qkv_kernel, nc=512),
        grid=(bsz, l // tm),
        in_specs=[tok, _mod_block(layer, 1, d, ctx_row), _mod_block(layer, 0, d, ctx_row),
                  _layer_block(slot, (d, 3 * d)), _layer_block(slot, (1, 3 * d))],
        out_specs=[tok, tok, tok],
        out_shape=[out, out, out],
        compiler_params=_params(2),
        name="modulate_qkv",
    )(x, ada, ada, w, b)


def _bias_patterns(rows):
    last = rows - Q_ROWS
    return ((0, 0), (Q_ROWS, Q_ROWS - WIN_H // 2), (last, rows - KEY_ROWS))


def _bias_kernel(t_ref, o_ref, *, rows):
    p = pl.program_id(2)
    for pat, (qrow0, w0) in enumerate(_bias_patterns(rows)):
        @pl.when(p == pat)
        def _():
            for i in range(Q_ROWS):
                r = qrow0 + i
                r0 = min(max(r - WIN_H // 2, 0), rows - WIN_H)
                for j in range(KEY_ROWS):
                    kr = w0 + j
                    lane0 = (j % 2) * GRID_W
                    dst = (slice(i * GRID_W, (i + 1) * GRID_W), slice(j * GRID_W, (j + 1) * GRID_W))
                    if r0 <= kr < r0 + WIN_H:
                        o_ref[dst] = t_ref[kr - r + WIN_H - 1, :, lane0:lane0 + GRID_W] * LOG2E
                    else:
                        o_ref[dst] = jnp.full((GRID_W, GRID_W), NEG_INF, F32)


def _bias_tables(rpb, rows):
    n_layers, nh, nd, no = rpb.shape
    qc = np.arange(GRID_W)[:, None]
    kc = np.arange(GRID_W)[None, :]
    qstart = np.clip(qc - WIN_W // 2, 0, GRID_W - WIN_W)
    col_ok = (kc >= qstart) & (kc < qstart + WIN_W)
    off = kc - qc + WIN_W - 1
    onehot = jnp.asarray(((off[None] == np.arange(no)[:, None, None]) & col_ok[None]).astype(np.float32))
    toep = jnp.einsum('lhdo,oqk->lhdqk', rpb, onehot, precision=lax.Precision.HIGHEST)
    toep = jnp.where(jnp.asarray(col_ok), toep, NEG_INF)
    toep = jnp.concatenate([toep, toep], axis=-1)
    tq, tk = Q_ROWS * GRID_W, KEY_ROWS * GRID_W
    return pl.pallas_call(
        functools.partial(_bias_kernel, rows=rows),
        grid=(n_layers, nh, 3),
        in_specs=[pl.BlockSpec((None, None, nd, GRID_W, 2 * GRID_W), lambda a, h, p: (a, h, 0, 0, 0))],
        out_specs=pl.BlockSpec((None, None, None, tq, tk), lambda a, h, p: (a, p, h, 0, 0)),
        out_shape=jax.ShapeDtypeStruct((n_layers, 3, nh, tq, tk), F32),
        compiler_params=_params(3),
        name="attn_bias_table",
    )(toep)


def _nattn_kernel(q_ref, k_ref, v_ref, kc_ref, vc_ref, bias_ref, o_ref, *, rows):
    gs = pl.program_id(2)
    n_steps = pl.num_programs(2)
    lanes = q_ref.shape[-1]
    tq = Q_ROWS * GRID_W
    n_keys = KEY_ROWS * GRID_W
    lane = lax.broadcasted_iota(jnp.int32, (1, lanes), 1)
    kc = kc_ref[...]
    vc = vc_ref[...]
    for sg in range(GROUPS_PER_STEP):
        g = gs * GROUPS_PER_STEP + sg
        if sg == 0:
            pat = jnp.where(gs == 0, 0, 1)
        elif sg == GROUPS_PER_STEP - 1:
            pat = jnp.where(gs == n_steps - 1, 2, 1)
        else:
            pat = 1
        w0 = jnp.clip(g * Q_ROWS - WIN_H // 2, 0, rows - KEY_ROWS)
        start = pl.multiple_of(w0 * GRID_W, (WIN_H // 2) * GRID_W)
        q = q_ref[sg * tq:(sg + 1) * tq, :]
        kw = k_ref[pl.ds(start, n_keys), :]
        vall = jnp.concatenate([v_ref[pl.ds(start, n_keys), :], vc], axis=0)
        vext = jnp.concatenate([vall, jnp.ones_like(vall)], axis=1)
        qq = jnp.concatenate([jnp.where((lane // HEAD_DIM) == h, q, jnp.zeros_like(q))
                              for h in range(HEADS_PER_STEP)], axis=0)
        s = _dot_nt(qq, kw) + bias_ref[pat].reshape(HEADS_PER_STEP * tq, n_keys)
        sc = _dot_nt(qq, kc)
        m = jnp.maximum(jnp.max(s, axis=-1, keepdims=True), jnp.max(sc, axis=-1, keepdims=True))
        p = jnp.concatenate([jnp.exp2(s - m).astype(BF16), jnp.exp2(sc - m).astype(BF16)], axis=1)
        o = _dot(p, vext)
        o = o[:, :lanes] / o[:, lanes:]
        o_ref[sg * tq:(sg + 1) * tq, :] = jnp.where(lane < HEAD_DIM, o[:tq], o[tq:]).astype(BF16)


def _nattn(q, k, v, kc, vc, bias, slot):
    bsz, n, d = q.shape
    cl = kc.shape[1]
    rows = n // GRID_W
    lanes = HEADS_PER_STEP * HEAD_DIM
    tq = Q_ROWS * GRID_W
    step_rows = GROUPS_PER_STEP * Q_ROWS
    qspec = pl.BlockSpec((None, step_rows * GRID_W, lanes), lambda hp, b, g: (b, g, hp))
    kvspec = pl.BlockSpec((None, n, lanes), lambda hp, b, g: (b, 0, hp))
    cspec = pl.BlockSpec((None, cl, lanes), lambda hp, b, g: (b, 0, hp))
    bspec = pl.BlockSpec((None, 3, HEADS_PER_STEP, tq, KEY_ROWS * GRID_W), lambda hp, b, g: (slot, 0, hp, 0, 0),
                         pipeline_mode=pl.Buffered(1))
    return pl.pallas_call(
        functools.partial(_nattn_kernel, rows=rows),
        grid=(d // lanes, bsz, rows // step_rows),
        in_specs=[qspec, kvspec, kvspec, cspec, cspec, bspec],
        out_specs=qspec,
        out_shape=jax.ShapeDtypeStruct((bsz, n, d), BF16),
        compiler_params=_params(3),
        name="neighborhood_attention",
    )(q, k, v, kc, vc, bias)


def _cattn_kernel(q_ref, k_ref, v_ref, o_ref):
    outs = []
    for h in range(HEADS_PER_STEP):
        sl = slice(h * HEAD_DIM, (h + 1) * HEAD_DIM)
        s = _dot_nt(q_ref[:, sl], k_ref[:, sl])
        p = jnp.exp2(s - jnp.max(s, axis=-1, keepdims=True))
        denom = jnp.sum(p, axis=-1, keepdims=True)
        outs.append(_dot(p.astype(BF16), v_ref[:, sl]) / denom)
    o_ref[...] = jnp.concatenate(outs, axis=-1).astype(BF16)


def _cattn(q, k, v):
    bsz, cl, d = q.shape
    lanes = HEADS_PER_STEP * HEAD_DIM
    spec = pl.BlockSpec((None, cl, lanes), lambda b, hp: (b, 0, hp))
    return pl.pallas_call(
        _cattn_kernel,
        grid=(bsz, d // lanes),
        in_specs=[spec, spec, spec],
        out_specs=spec,
        out_shape=jax.ShapeDtypeStruct((bsz, cl, d), BF16),
        compiler_params=_params(2),
        name="context_attention",
    )(q, k, v)


def _proj_ln_kernel(o_ref, w_ref, b_ref, x_ref, g_ref, lg_ref, lb_ref, out_ref, *, alpha):
    y = _dot(o_ref[...], w_ref[...]) + b_ref[...]
    z = alpha * x_ref[...] + g_ref[...] * y
    out_ref[...] = _layer_norm(z, lg_ref[...], lb_ref[...])


def _proj_ln(o, w, b, slot, x, ada, layer, ctx_row, ln_g, ln_b, alpha, tm):
    bsz, l, d = x.shape
    tok = pl.BlockSpec((None, tm, d), lambda i, j: (i, j, 0))
    return pl.pallas_call(
        functools.partial(_proj_ln_kernel, alpha=alpha),
        grid=(bsz, l // tm),
        in_specs=[tok, _layer_block(slot, (d, d)), _layer_block(slot, (1, d)), tok,
                  _mod_block(layer, 2, d, ctx_row), _layer_block(layer, (1, d)), _layer_block(layer, (1, d))],
        out_specs=tok,
        out_shape=jax.ShapeDtypeStruct((bsz, l, d), F32),
        compiler_params=_params(2),
        name="out_proj_deepnorm",
    )(o, w, b, x, ada, ln_g, ln_b)


def _ffn_chunks(d_ff, chunk=512):
    edges = list(range(0, d_ff, chunk)) + [d_ff]
    return list(zip(edges[:-1], edges[1:]))


def _ffn_kernel(x_ref, sc_ref, sh_ref, g_ref, w1_ref, w3_ref, w2_ref, lg_ref, lb_ref, out_ref, *, alpha):
    x = x_ref[...]
    h = _modulate(x, sc_ref[...], sh_ref[...])
    acc = None
    for lo, hi in _ffn_chunks(w1_ref.shape[1]):
        a = _dot(h, w1_ref[:, lo:hi])
        b = _dot(h, w3_ref[:, lo:hi])
        act = (a * jax.nn.sigmoid(a) * b).astype(BF16)
        part = _dot(act, w2_ref[lo:hi, :])
        acc = part if acc is None else acc + part
    z = alpha * x + g_ref[...] * acc
    out_ref[...] = _layer_norm(z, lg_ref[...], lb_ref[...])


def _ffn(x, ada, layer, ctx_row, w1, w3, w2, ln_g, ln_b, alpha, tm):
    bsz, l, d = x.shape
    d_ff = w1.shape[-1]
    tok = pl.BlockSpec((None, tm, d), lambda i, j: (i, j, 0))
    return pl.pallas_call(
        functools.partial(_ffn_kernel, alpha=alpha),
        grid=(bsz, l // tm),
        in_specs=[tok, _mod_block(layer, 4, d, ctx_row), _mod_block(layer, 3, d, ctx_row),
                  _mod_block(layer, 5, d, ctx_row),
                  _layer_block(layer, (d, d_ff)), _layer_block(layer, (d, d_ff)), _layer_block(layer, (d_ff, d)),
                  _layer_block(layer, (1, d)), _layer_block(layer, (1, d))],
        out_specs=tok,
        out_shape=jax.ShapeDtypeStruct((bsz, l, d), F32),
        compiler_params=_params(2),
        name="swiglu_deepnorm",
    )(x, ada, ada, ada, w1, w3, w2, ln_g, ln_b)


def _glu_kernel(x_ref, sc_ref, sh_ref, w_ref, b_ref, u_ref, *, nc):
    d = x_ref.shape[-1]
    h = _modulate(x_ref[...], sc_ref[...], sh_ref[...])
    for n in range(d // nc):
        lo = n * nc
        a = _dot(h, w_ref[:, lo:lo + nc]) + b_ref[:, lo:lo + nc]
        g = _dot(h, w_ref[:, d + lo:d + lo + nc]) + b_ref[:, d + lo:d + lo + nc]
        u_ref[:, lo:lo + nc] = a * jax.nn.sigmoid(g)


def _glu(x, ada, layer, ctx_row, w, b, slot, tm):
    bsz, l, d = x.shape
    tok = pl.BlockSpec((None, tm, d), lambda i, j: (i, j, 0))
    return pl.pallas_call(
        functools.partial(_glu_kernel, nc=512),
        grid=(bsz, l // tm),
        in_specs=[tok, _mod_block(layer, 1, d, ctx_row), _mod_block(layer, 0, d, ctx_row),
                  _layer_block(slot, (d, 2 * d)), _layer_block(slot, (1, 2 * d))],
        out_specs=tok,
        out_shape=jax.ShapeDtypeStruct((bsz, l, d), F32),
        compiler_params=_params(2),
        name="modulate_pw1_glu",
    )(x, ada, ada, w, b)


def _conv_kernel(up_ref, u_ref, un_ref, wdw_ref, bdw_ref, cg_ref, cb_ref, w2_ref, b2_ref,
                 x_ref, g_ref, lg_ref, lb_ref, out_ref, ext_ref, y_ref, *, alpha, rc, lc):
    j = pl.program_id(1)
    tm, d = u_ref.shape
    pad = CONV_K // 2
    ext_ref[0:HALO, :] = jnp.where(j == 0, 0.0, up_ref[...])
    ext_ref[HALO:HALO + tm, :] = u_ref[...]
    ext_ref[HALO + tm:HALO + tm + HALO, :] = jnp.where(j == pl.num_programs(1) - 1, 0.0, un_ref[...])

    span = HALO - pad + CONV_K - 1
    span8 = -(-span // 8) * 8

    def row_chunk(c, carry):
        r0 = pl.multiple_of(c * rc, rc)
        for l0 in range(0, d, lc):
            win = ext_ref[pl.ds(r0, rc + span8), l0:l0 + lc]
            acc = jnp.zeros((rc // 8, 8, lc), F32)
            for r in range(8):
                shifted = win if r == 0 else pltpu.roll(win, rc + span8 - r, axis=0)
                shifted = shifted.reshape((rc + span8) // 8, 8, lc)
                for a in range(span8 // 8):
                    k = 8 * a + r - (HALO - pad)
                    if 0 <= k < CONV_K:
                        acc = acc + shifted[a:a + rc // 8] * wdw_ref[k, :, l0:l0 + lc][None]
            y_ref[pl.ds(r0, rc), l0:l0 + lc] = acc.reshape(rc, lc)
        return carry

    lax.fori_loop(0, tm // rc, row_chunk, 0)
    t = _layer_norm(y_ref[...] + bdw_ref[...], cg_ref[...], cb_ref[...])
    t = (t * jax.nn.sigmoid(t)).astype(BF16)
    y = _dot(t, w2_ref[...]) + b2_ref[...]
    z = alpha * x_ref[...] + g_ref[...] * y
    out_ref[...] = _layer_norm(z, lg_ref[...], lb_ref[...])


def _conv(u, w_dw8, b_dw, cg, cb, w2, b2, slot, x, ada, layer, ctx_row, ln_g, ln_b, alpha, tm):
    bsz, l, d = x.shape
    nh = tm // HALO
    last = l // HALO - 1
    tok = pl.BlockSpec((None, tm, d), lambda i, j: (i, j, 0))
    prev = pl.BlockSpec((None, HALO, d), lambda i, j: (i, jnp.maximum(j * nh - 1, 0), 0))
    nxt = pl.BlockSpec((None, HALO, d), lambda i, j: (i, jnp.minimum((j + 1) * nh, last), 0))
    srow = _layer_block(slot, (1, d))
    lrow = _layer_block(layer, (1, d))
    return pl.pallas_call(
        functools.partial(_conv_kernel, alpha=alpha, rc=64, lc=128),
        grid=(bsz, l // tm),
        in_specs=[prev, tok, nxt, _layer_block(slot, (CONV_K, 8, d)), srow, srow, srow,
                  _layer_block(slot, (d, d)), srow, tok, _mod_block(layer, 2, d, ctx_row), lrow, lrow],
        out_specs=tok,
        out_shape=jax.ShapeDtypeStruct((bsz, l, d), F32),
        scratch_shapes=[pltpu.VMEM((tm + 2 * HALO, d), F32), pltpu.VMEM((tm, d), F32)],
        compiler_params=_params(2),
        name="dwconv_pw2_deepnorm",
    )(u, u, u, w_dw8, b_dw, cg, cb, w2, b2, x, ada, ln_g, ln_b)


def kernel(x, c, ctx, c_ctx, w_ada, b_ada, ln_mix_g, ln_mix_b, ln_ffn_g, ln_ffn_b, attn_w_qkv, attn_b_qkv, attn_w_o, attn_b_o, attn_rpb, conv_w_pw1, conv_b_pw1, conv_w_dw, conv_b_dw, conv_ln_g, conv_ln_b, conv_w_pw2, conv_b_pw2, ffn_w1, ffn_w3, ffn_w2):
    bsz, n, d = x.shape
    cl = ctx.shape[1]
    depth = w_ada.shape[0]
    rows = n // GRID_W
    assert d == N_HEADS * HEAD_DIM and HEADS_PER_STEP == 2 and bsz < ADA_ROWS
    assert n % (GROUPS_PER_STEP * Q_ROWS * GRID_W) == 0 and rows >= 2 * KEY_ROWS
    alpha = (2 * depth) ** 0.25
    last_attn = max(i for i in range(depth) if i % N_MIXERS == 0)
    tm = min(512, n)
    tmc = min(512, cl)
    ctx_row = bsz

    s_rows = jnp.zeros((ADA_ROWS, d), F32).at[:bsz].set(c).at[ctx_row].set(c_ctx)
    ada = _ada_all(s_rows, w_ada, b_ada).reshape(depth, ADA_ROWS, 1, N_MODS * d)

    bf = lambda w: w.astype(BF16)
    rows3 = lambda v: v.reshape(v.shape[0], 1, v.shape[1])
    w1, w3, w2 = bf(ffn_w1), bf(ffn_w3), bf(ffn_w2)
    wqkv, wo, wp1, wp2 = bf(attn_w_qkv), bf(attn_w_o), bf(conv_w_pw1), bf(conv_w_pw2)
    bqkv, bo, bp1, bp2 = rows3(attn_b_qkv), rows3(attn_b_o), rows3(conv_b_pw1), rows3(conv_b_pw2)
    lmg, lmb, lfg, lfb = rows3(ln_mix_g), rows3(ln_mix_b), rows3(ln_ffn_g), rows3(ln_ffn_b)
    conv_args = (jnp.broadcast_to(conv_w_dw[:, :, None, :], conv_w_dw.shape[:2] + (8, d)),
                 rows3(conv_b_dw), rows3(conv_ln_g), rows3(conv_ln_b), wp2, bp2)
    bias = _bias_tables(attn_rpb, rows)

    xc = ctx
    for i in range(depth):
        slot = i // N_MIXERS
        ctx_live = i < last_attn
        if i % N_MIXERS == 0:
            q, k, v = _qkv(x, ada, i, None, wqkv, bqkv, slot, tm)
            qc, kc, vc = _qkv(xc, ada, i, ctx_row, wqkv, bqkv, slot, tmc)
            o = _nattn(q, k, v, kc, vc, bias, slot)
            x = _proj_ln(o, wo, bo, slot, x, ada, i, None, lmg, lmb, alpha, tm)
            if ctx_live:
                oc = _cattn(qc, kc, vc)
                xc = _proj_ln(oc, wo, bo, slot, xc, ada, i, ctx_row, lmg, lmb, alpha, tmc)
        else:
            u = _glu(x, ada, i, None, wp1, bp1, slot, tm)
            x = _conv(u, *conv_args, slot, x, ada, i, None, lmg, lmb, alpha, tm)
            if ctx_live:
                uc = _glu(xc, ada, i, ctx_row, wp1, bp1, slot, tmc)
                xc = _conv(uc, *conv_args, slot, xc, ada, i, ctx_row, lmg, lmb, alpha, tmc)
        x = _ffn(x, ada, i, None, w1, w3, w2, lfg, lfb, alpha, tm)
        if ctx_live:
            xc = _ffn(xc, ada, i, ctx_row, w1, w3, w2, lfg, lfb, alpha, tmc)
    return x
```

```python
import functools

import numpy as np
import jax
import jax.numpy as jnp
from jax import lax
from jax.experimental import pallas as pl
from jax.experimental.pallas import tpu as pltpu

F32 = jnp.float32
BF16 = jnp.bfloat16

N_HEADS = 16
HEAD_DIM = 64
GRID_W = 64
WIN_H = 8
WIN_W = 16
CONV_K = 31
N_MIXERS = 2
N_MODS = 6
LN_EPS = 1e-5
NEG_INF = -1e30

LOG2E = 1.4426950408889634
Q_ROWS = 4
KEY_ROWS = Q_ROWS + WIN_H
GROUPS_PER_STEP = 16
HEADS_PER_STEP = 2
HALO = 16
ADA_ROWS = 16
VMEM_LIMIT = 56 * 1024 * 1024


def _params(n_axes):
    return pltpu.CompilerParams(dimension_semantics=("arbitrary",) * n_axes,
                                vmem_limit_bytes=VMEM_LIMIT)


def _layer_block(layer, shape):
    return pl.BlockSpec((None,) + tuple(shape), lambda *_: (layer,) + (0,) * len(shape),
                        pipeline_mode=pl.Buffered(1))


def _mod_block(layer, chunk, d, ctx_row):
    if ctx_row is None:
        return pl.BlockSpec((None, None, 1, d), lambda i, j: (layer, i, 0, chunk))
    return pl.BlockSpec((None, None, 1, d), lambda i, j: (layer, ctx_row, 0, chunk))


def _layer_norm(z, g, b):
    mu = jnp.mean(z, axis=-1, keepdims=True)
    zc = z - mu
    var = jnp.mean(zc * zc, axis=-1, keepdims=True)
    return zc * lax.rsqrt(var + LN_EPS) * g + b


def _modulate(x, sc, sh):
    return (x * (1.0 + sc) + sh).astype(BF16)


def _dot(a, b):
    return jnp.dot(a, b, preferred_element_type=F32)


def _dot_nt(a, b):
    return lax.dot_general(a, b, (((1,), (1,)), ((), ())), preferred_element_type=F32)


def _ada_kernel(s_ref, w_ref, b_ref, o_ref):
    s = s_ref[...]
    s = s * jax.nn.sigmoid(s)
    o_ref[...] = _dot(s.astype(BF16), w_ref[...].astype(BF16)) + b_ref[...]


def _ada_all(s_rows, w_ada, b_ada, tn=1536):
    depth, d, n = w_ada.shape
    rows = s_rows.shape[0]
    return pl.pallas_call(
        _ada_kernel,
        grid=(depth, n // tn),
        in_specs=[pl.BlockSpec((rows, d), lambda i, j: (0, 0)),
                  pl.BlockSpec((None, d, tn), lambda i, j: (i, 0, j)),
                  pl.BlockSpec((None, 1, tn), lambda i, j: (i, 0, j))],
        out_specs=pl.BlockSpec((None, rows, tn), lambda i, j: (i, 0, j)),
        out_shape=jax.ShapeDtypeStruct((depth, rows, n), F32),
        compiler_params=_params(2),
        name="ada_params",
    )(s_rows, w_ada, b_ada.reshape(depth, 1, n))


def _qkv_kernel(x_ref, sc_ref, sh_ref, w_ref, b_ref, q_ref, k_ref, v_ref, *, nc):
    d = x_ref.shape[-1]
    h = _modulate(x_ref[...], sc_ref[...], sh_ref[...])
    for idx, o_ref in enumerate((q_ref, k_ref, v_ref)):
        for n in range(d // nc):
            col = idx * d + n * nc
            y = _dot(h, w_ref[:, col:col + nc]) + b_ref[:, col:col + nc]
            if idx == 0:
                y = y * (HEAD_DIM ** -0.5 * LOG2E)
            o_ref[:, n * nc:(n + 1) * nc] = y.astype(BF16)


def _qkv(x, ada, layer, ctx_row, w, b, slot, tm):
    bsz, l, d = x.shape
    tok = pl.BlockSpec((None, tm, d), lambda i, j: (i, j, 0))
    out = jax.ShapeDtypeStruct((bsz, l, d), BF16)
    return pl.pallas_call(
        functools.partial(_qkv_kernel, nc=512),
        grid=(bsz, l // tm),
        in_specs=[tok, _mod_block(layer, 1, d, ctx_row), _mod_block(layer, 0, d, ctx_row),
                  _layer_block(slot, (d, 3 * d)), _layer_block(slot, (1, 3 * d))],
        out_specs=[tok, tok, tok],
        out_shape=[out, out, out],
        compiler_params=_params(2),
        name="modulate_qkv",
    )(x, ada, ada, w, b)


def _bias_patterns(rows):
    last = rows - Q_ROWS
    return ((0, 0), (Q_ROWS, Q_ROWS - WIN_H // 2), (last, rows - KEY_ROWS))


def _bias_kernel(t_ref, o_ref, *, rows):
    for pat, (qrow0, w0) in enumerate(_bias_patterns(rows)):
        for i in range(Q_ROWS):
            r = qrow0 + i
            r0 = min(max(r - WIN_H // 2, 0), rows - WIN_H)
            for j in range(KEY_ROWS):
                kr = w0 + j
                lane0 = (j % 2) * GRID_W
                dst = (pat, slice(i * GRID_W, (i + 1) * GRID_W), slice(j * GRID_W, (j + 1) * GRID_W))
                if r0 <= kr < r0 + WIN_H:
                    o_ref[dst] = t_ref[kr - r + WIN_H - 1, :, lane0:lane0 + GRID_W] * LOG2E
                else:
                    o_ref[dst] = jnp.full((GRID_W, GRID_W), NEG_INF, F32)


def _bias_tables(rpb, rows):
    n_layers, nh, nd, no = rpb.shape
    qc = np.arange(GRID_W)[:, None]
    kc = np.arange(GRID_W)[None, :]
    qstart = np.clip(qc - WIN_W // 2, 0, GRID_W - WIN_W)
    col_ok = (kc >= qstart) & (kc < qstart + WIN_W)
    off = kc - qc + WIN_W - 1
    onehot = jnp.asarray(((off[None] == np.arange(no)[:, None, None]) & col_ok[None]).astype(np.float32))
    toep = jnp.einsum('lhdo,oqk->lhdqk', rpb, onehot, precision=lax.Precision.HIGHEST)
    toep = jnp.where(jnp.asarray(col_ok), toep, NEG_INF)
    toep = jnp.concatenate([toep, toep], axis=-1)
    tq, tk = Q_ROWS * GRID_W, KEY_ROWS * GRID_W
    return pl.pallas_call(
        functools.partial(_bias_kernel, rows=rows),
        grid=(n_layers, nh),
        in_specs=[pl.BlockSpec((None, None, nd, GRID_W, 2 * GRID_W), lambda a, h: (a, h, 0, 0, 0))],
        out_specs=pl.BlockSpec((None, 3, None, tq, tk), lambda a, h: (a, 0, h, 0, 0)),
        out_shape=jax.ShapeDtypeStruct((n_layers, 3, nh, tq, tk), F32),
        compiler_params=_params(2),
        name="attn_bias_table",
    )(toep)


def _nattn_kernel(q_ref, k_ref, v_ref, kc_ref, vc_ref, bias_ref, o_ref, *, rows):
    gs = pl.program_id(2)
    n_steps = pl.num_programs(2)
    lanes = q_ref.shape[-1]
    tq = Q_ROWS * GRID_W
    n_keys = KEY_ROWS * GRID_W
    lane = lax.broadcasted_iota(jnp.int32, (1, lanes), 1)
    kc = kc_ref[...]
    vc = vc_ref[...]
    for sg in range(GROUPS_PER_STEP):
        g = gs * GROUPS_PER_STEP + sg
        if sg == 0:
            pat = jnp.where(gs == 0, 0, 1)
        elif sg == GROUPS_PER_STEP - 1:
            pat = jnp.where(gs == n_steps - 1, 2, 1)
        else:
            pat = 1
        w0 = jnp.clip(g * Q_ROWS - WIN_H // 2, 0, rows - KEY_ROWS)
        start = pl.multiple_of(w0 * GRID_W, (WIN_H // 2) * GRID_W)
        q = q_ref[sg * tq:(sg + 1) * tq, :]
        kw = k_ref[pl.ds(start, n_keys), :]
        vall = jnp.concatenate([v_ref[pl.ds(start, n_keys), :], vc], axis=0)
        vext = jnp.concatenate([vall, jnp.ones_like(vall)], axis=1)
        qq = jnp.concatenate([jnp.where((lane // HEAD_DIM) == h, q, jnp.zeros_like(q))
                              for h in range(HEADS_PER_STEP)], axis=0)
        s = _dot_nt(qq, kw) + bias_ref[pat].reshape(HEADS_PER_STEP * tq, n_keys)
        sc = _dot_nt(qq, kc)
        m = jnp.maximum(jnp.max(s, axis=-1, keepdims=True), jnp.max(sc, axis=-1, keepdims=True))
        p = jnp.concatenate([jnp.exp2(s - m).astype(BF16), jnp.exp2(sc - m).astype(BF16)], axis=1)
        o = _dot(p, vext)
        o = o[:, :lanes] / o[:, lanes:]
        o_ref[sg * tq:(sg + 1) * tq, :] = jnp.where(lane < HEAD_DIM, o[:tq], o[tq:]).astype(BF16)


def _nattn(q, k, v, kc, vc, bias, slot):
    bsz, n, d = q.shape
    cl = kc.shape[1]
    rows = n // GRID_W
    lanes = HEADS_PER_STEP * HEAD_DIM
    tq = Q_ROWS * GRID_W
    step_rows = GROUPS_PER_STEP * Q_ROWS
    qspec = pl.BlockSpec((None, step_rows * GRID_W, lanes), lambda hp, b, g: (b, g, hp))
    kvspec = pl.BlockSpec((None, n, lanes), lambda hp, b, g: (b, 0, hp))
    cspec = pl.BlockSpec((None, cl, lanes), lambda hp, b, g: (b, 0, hp))
    bspec = pl.BlockSpec((None, 3, HEADS_PER_STEP, tq, KEY_ROWS * GRID_W), lambda hp, b, g: (slot, 0, hp, 0, 0),
                         pipeline_mode=pl.Buffered(1))
    return pl.pallas_call(
        functools.partial(_nattn_kernel, rows=rows),
        grid=(d // lanes, bsz, rows // step_rows),
        in_specs=[qspec, kvspec, kvspec, cspec, cspec, bspec],
        out_specs=qspec,
        out_shape=jax.ShapeDtypeStruct((bsz, n, d), BF16),
        compiler_params=_params(3),
        name="neighborhood_attention",
    )(q, k, v, kc, vc, bias)


def _cattn_kernel(q_ref, k_ref, v_ref, o_ref):
    lanes = HEADS_PER_STEP * HEAD_DIM
    for hp in range(q_ref.shape[-1] // lanes):
        outs = []
        for h in range(HEADS_PER_STEP):
            sl = slice(hp * lanes + h * HEAD_DIM, hp * lanes + (h + 1) * HEAD_DIM)
            s = _dot_nt(q_ref[:, sl], k_ref[:, sl])
            p = jnp.exp2(s - jnp.max(s, axis=-1, keepdims=True))
            denom = jnp.sum(p, axis=-1, keepdims=True)
            outs.append(_dot(p.astype(BF16), v_ref[:, sl]) / denom)
        o_ref[:, hp * lanes:(hp + 1) * lanes] = jnp.concatenate(outs, axis=-1).astype(BF16)


def _cattn(q, k, v):
    bsz, cl, d = q.shape
    spec = pl.BlockSpec((None, cl, d), lambda b: (b, 0, 0))
    return pl.pallas_call(
        _cattn_kernel,
        grid=(bsz,),
        in_specs=[spec, spec, spec],
        out_specs=spec,
        out_shape=jax.ShapeDtypeStruct((bsz, cl, d), BF16),
        compiler_params=_params(1),
        name="context_attention",
    )(q, k, v)


def _ffn_chunks(d_ff, chunk=512):
    edges = list(range(0, d_ff, chunk)) + [d_ff]
    return list(zip(edges[:-1], edges[1:]))


def _ffn_kernel(*refs, alpha, with_proj):
    if with_proj:
        o_ref, wo_ref, bo_ref, g1_ref, lmg_ref, lmb_ref, *refs = refs
    x_ref, sc_ref, sh_ref, g_ref, w1_ref, w3_ref, w2_ref, lg_ref, lb_ref, out_ref = refs
    x = x_ref[...]
    if with_proj:
        y = _dot(o_ref[...], wo_ref[...]) + bo_ref[...]
        x = _layer_norm(alpha * x + g1_ref[...] * y, lmg_ref[...], lmb_ref[...])
    h = _modulate(x, sc_ref[...], sh_ref[...])
    acc = None
    for lo, hi in _ffn_chunks(w1_ref.shape[1]):
        a = _dot(h, w1_ref[:, lo:hi])
        b = _dot(h, w3_ref[:, lo:hi])
        act = (a * jax.nn.sigmoid(a) * b).astype(BF16)
        part = _dot(act, w2_ref[lo:hi, :])
        acc = part if acc is None else acc + part
    z = alpha * x + g_ref[...] * acc
    out_ref[...] = _layer_norm(z, lg_ref[...], lb_ref[...])


def _ffn(x, ada, layer, ctx_row, w1, w3, w2, ln_g, ln_b, alpha, tm, proj=None):
    bsz, l, d = x.shape
    d_ff = w1.shape[-1]
    tok = pl.BlockSpec((None, tm, d), lambda i, j: (i, j, 0))
    lrow = _layer_block(layer, (1, d))
    specs = [tok, _mod_block(layer, 4, d, ctx_row), _mod_block(layer, 3, d, ctx_row),
             _mod_block(layer, 5, d, ctx_row),
             _layer_block(layer, (d, d_ff)), _layer_block(layer, (d, d_ff)), _layer_block(layer, (d_ff, d)),
             lrow, lrow]
    args = [x, ada, ada, ada, w1, w3, w2, ln_g, ln_b]
    if proj is not None:
        o, wo, bo, slot, lmg, lmb = proj
        specs = [tok, _layer_block(slot, (d, d)), _layer_block(slot, (1, d)),
                 _mod_block(layer, 2, d, ctx_row), lrow, lrow] + specs
        args = [o, wo, bo, ada, lmg, lmb] + args
    return pl.pallas_call(
        functools.partial(_ffn_kernel, alpha=alpha, with_proj=proj is not None),
        grid=(bsz, l // tm),
        in_specs=specs,
        out_specs=tok,
        out_shape=jax.ShapeDtypeStruct((bsz, l, d), F32),
        compiler_params=_params(2),
        name="outproj_swiglu_deepnorm" if proj is not None else "swiglu_deepnorm",
    )(*args)


def _glu_kernel(x_ref, sc_ref, sh_ref, w_ref, b_ref, u_ref, *, nc):
    d = x_ref.shape[-1]
    h = _modulate(x_ref[...], sc_ref[...], sh_ref[...])
    for n in range(d // nc):
        lo = n * nc
        a = _dot(h, w_ref[:, lo:lo + nc]) + b_ref[:, lo:lo + nc]
        g = _dot(h, w_ref[:, d + lo:d + lo + nc]) + b_ref[:, d + lo:d + lo + nc]
        u_ref[:, lo:lo + nc] = a * jax.nn.sigmoid(g)


def _glu(x, ada, layer, ctx_row, w, b, slot, tm):
    bsz, l, d = x.shape
    tok = pl.BlockSpec((None, tm, d), lambda i, j: (i, j, 0))
    return pl.pallas_call(
        functools.partial(_glu_kernel, nc=512),
        grid=(bsz, l // tm),
        in_specs=[tok, _mod_block(layer, 1, d, ctx_row), _mod_block(layer, 0, d, ctx_row),
                  _layer_block(slot, (d, 2 * d)), _layer_block(slot, (1, 2 * d))],
        out_specs=tok,
        out_shape=jax.ShapeDtypeStruct((bsz, l, d), F32),
        compiler_params=_params(2),
        name="modulate_pw1_glu",
    )(x, ada, ada, w, b)


def _conv_kernel(up_ref, u_ref, un_ref, wdw_ref, bdw_ref, cg_ref, cb_ref, w2_ref, b2_ref,
                 x_ref, g_ref, lg_ref, lb_ref, out_ref, ext_ref, y_ref, *, alpha, rc, lc):
    j = pl.program_id(1)
    tm, d = u_ref.shape
    pad = CONV_K // 2
    ext_ref[0:HALO, :] = jnp.where(j == 0, 0.0, up_ref[...])
    ext_ref[HALO:HALO + tm, :] = u_ref[...]
    ext_ref[HALO + tm:HALO + tm + HALO, :] = jnp.where(j == pl.num_programs(1) - 1, 0.0, un_ref[...])

    span = HALO - pad + CONV_K - 1
    span8 = -(-span // 8) * 8

    def row_chunk(c, carry):
        r0 = pl.multiple_of(c * rc, rc)
        for l0 in range(0, d, lc):
            win = ext_ref[pl.ds(r0, rc + span8), l0:l0 + lc]
            acc = jnp.zeros((rc // 8, 8, lc), F32)
            for r in range(8):
                shifted = win if r == 0 else pltpu.roll(win, rc + span8 - r, axis=0)
                shifted = shifted.reshape((rc + span8) // 8, 8, lc)
                for a in range(span8 // 8):
                    k = 8 * a + r - (HALO - pad)
                    if 0 <= k < CONV_K:
                        acc = acc + shifted[a:a + rc // 8] * wdw_ref[k, :, l0:l0 + lc][None]
            y_ref[pl.ds(r0, rc), l0:l0 + lc] = acc.reshape(rc, lc)
        return carry

    lax.fori_loop(0, tm // rc, row_chunk, 0)
    t = _layer_norm(y_ref[...] + bdw_ref[...], cg_ref[...], cb_ref[...])
    t = (t * jax.nn.sigmoid(t)).astype(BF16)
    y = _dot(t, w2_ref[...]) + b2_ref[...]
    z = alpha * x_ref[...] + g_ref[...] * y
    out_ref[...] = _layer_norm(z, lg_ref[...], lb_ref[...])


def _conv(u, w_dw8, b_dw, cg, cb, w2, b2, slot, x, ada, layer, ctx_row, ln_g, ln_b, alpha, tm):
    bsz, l, d = x.shape
    nh = tm // HALO
    last = l // HALO - 1
    tok = pl.BlockSpec((None, tm, d), lambda i, j: (i, j, 0))
    prev = pl.BlockSpec((None, HALO, d), lambda i, j: (i, jnp.maximum(j * nh - 1, 0), 0))
    nxt = pl.BlockSpec((None, HALO, d), lambda i, j: (i, jnp.minimum((j + 1) * nh, last), 0))
    srow = _layer_block(slot, (1, d))
    lrow = _layer_block(layer, (1, d))
    return pl.pallas_call(
        functools.partial(_conv_kernel, alpha=alpha, rc=64, lc=128),
        grid=(bsz, l // tm),
        in_specs=[prev, tok, nxt, _layer_block(slot, (CONV_K, 8, d)), srow, srow, srow,
                  _layer_block(slot, (d, d)), srow, tok, _mod_block(layer, 2, d, ctx_row), lrow, lrow],
        out_specs=tok,
        out_shape=jax.ShapeDtypeStruct((bsz, l, d), F32),
        scratch_shapes=[pltpu.VMEM((tm + 2 * HALO, d), F32), pltpu.VMEM((tm, d), F32)],
        compiler_params=_params(2),
        name="dwconv_pw2_deepnorm",
    )(u, u, u, w_dw8, b_dw, cg, cb, w2, b2, x, ada, ln_g, ln_b)


def kernel(x, c, ctx, c_ctx, w_ada, b_ada, ln_mix_g, ln_mix_b, ln_ffn_g, ln_ffn_b, attn_w_qkv, attn_b_qkv, attn_w_o, attn_b_o, attn_rpb, conv_w_pw1, conv_b_pw1, conv_w_dw, conv_b_dw, conv_ln_g, conv_ln_b, conv_w_pw2, conv_b_pw2, ffn_w1, ffn_w3, ffn_w2):
    bsz, n, d = x.shape
    cl = ctx.shape[1]
    depth = w_ada.shape[0]
    rows = n // GRID_W
    assert d == N_HEADS * HEAD_DIM and HEADS_PER_STEP == 2 and bsz < ADA_ROWS
    assert n % (GROUPS_PER_STEP * Q_ROWS * GRID_W) == 0 and rows >= 2 * KEY_ROWS
    alpha = (2 * depth) ** 0.25
    last_attn = max(i for i in range(depth) if i % N_MIXERS == 0)
    tm = min(512, n)
    tmc = min(512, cl)
    ctx_row = bsz

    s_rows = jnp.zeros((ADA_ROWS, d), F32).at[:bsz].set(c).at[ctx_row].set(c_ctx)
    ada = _ada_all(s_rows, w_ada, b_ada).reshape(depth, ADA_ROWS, 1, N_MODS * d)

    bf = lambda w: w.astype(BF16)
    rows3 = lambda v: v.reshape(v.shape[0], 1, v.shape[1])
    w1, w3, w2 = bf(ffn_w1), bf(ffn_w3), bf(ffn_w2)
    wqkv, wo, wp1, wp2 = bf(attn_w_qkv), bf(attn_w_o), bf(conv_w_pw1), bf(conv_w_pw2)
    bqkv, bo, bp1, bp2 = rows3(attn_b_qkv), rows3(attn_b_o), rows3(conv_b_pw1), rows3(conv_b_pw2)
    lmg, lmb, lfg, lfb = rows3(ln_mix_g), rows3(ln_mix_b), rows3(ln_ffn_g), rows3(ln_ffn_b)
    conv_args = (jnp.broadcast_to(conv_w_dw[:, :, None, :], conv_w_dw.shape[:2] + (8, d)),
                 rows3(conv_b_dw), rows3(conv_ln_g), rows3(conv_ln_b), wp2, bp2)
    bias = _bias_tables(attn_rpb, rows)

    xc = ctx
    for i in range(depth):
        slot = i // N_MIXERS
        ctx_live = i < last_attn
        if i % N_MIXERS == 0:
            q, k, v = _qkv(x, ada, i, None, wqkv, bqkv, slot, tm)
            qc, kc, vc = _qkv(xc, ada, i, ctx_row, wqkv, bqkv, slot, tmc)
            proj = (_nattn(q, k, v, kc, vc, bias, slot), wo, bo, slot, lmg, lmb)
            proj_c = (_cattn(qc, kc, vc), wo, bo, slot, lmg, lmb) if ctx_live else None
        else:
            proj = proj_c = None
            u = _glu(x, ada, i, None, wp1, bp1, slot, tm)
            x = _conv(u, *conv_args, slot, x, ada, i, None, lmg, lmb, alpha, tm)
            if ctx_live:
                uc = _glu(xc, ada, i, ctx_row, wp1, bp1, slot, tmc)
                xc = _conv(uc, *conv_args, slot, xc, ada, i, ctx_row, lmg, lmb, alpha, tmc)
        x = _ffn(x, ada, i, None, w1, w3, w2, lfg, lfb, alpha, tm, proj)
        if ctx_live:
            xc = _ffn(xc, ada, i, ctx_row, w1, w3, w2, lfg, lfb, alpha, tmc, proj_c)
    return x
```

```python
import functools

import numpy as np
import jax
import jax.numpy as jnp
from jax import lax
from jax.experimental import pallas as pl
from jax.experimental.pallas import tpu as pltpu

F32 = jnp.float32
BF16 = jnp.bfloat16

N_HEADS = 16
HEAD_DIM = 64
GRID_W = 64
WIN_H = 8
WIN_W = 16
CONV_K = 31
N_MIXERS = 2
N_MODS = 6
LN_EPS = 1e-5
NEG_INF = -1e30

LOG2E = 1.4426950408889634
Q_ROWS = 4
KEY_ROWS = Q_ROWS + WIN_H
GROUPS_PER_STEP = 16
KEY_TILE = 512
HEADS_PER_STEP = 2
HALO = 16
ADA_ROWS = 16
VMEM_LIMIT = 56 * 1024 * 1024


def _params(n_axes):
    return pltpu.CompilerParams(dimension_semantics=("arbitrary",) * n_axes,
                                vmem_limit_bytes=VMEM_LIMIT)


def _layer_block(layer, shape):
    return pl.BlockSpec((None,) + tuple(shape), lambda *_: (layer,) + (0,) * len(shape),
                        pipeline_mode=pl.Buffered(1))


def _mod_block(layer, chunk, d, ctx_row):
    if ctx_row is None:
        return pl.BlockSpec((None, None, 1, d), lambda i, j: (layer, i, 0, chunk))
    return pl.BlockSpec((None, None, 1, d), lambda i, j: (layer, ctx_row, 0, chunk))


def _layer_norm(z, g, b):
    mu = jnp.mean(z, axis=-1, keepdims=True)
    zc = z - mu
    var = jnp.mean(zc * zc, axis=-1, keepdims=True)
    return zc * lax.rsqrt(var + LN_EPS) * g + b


def _modulate(x, sc, sh):
    return (x * (1.0 + sc) + sh).astype(BF16)


def _dot(a, b):
    return jnp.dot(a, b, preferred_element_type=F32)


def _dot_nt(a, b):
    return lax.dot_general(a, b, (((1,), (1,)), ((), ())), preferred_element_type=F32)


def _ada_kernel(s_ref, w_ref, b_ref, o_ref):
    s = s_ref[...]
    s = s * jax.nn.sigmoid(s)
    o_ref[...] = _dot(s.astype(BF16), w_ref[...].astype(BF16)) + b_ref[...]


def _ada_all(s_rows, w_ada, b_ada, tn=1536):
    depth, d, n = w_ada.shape
    rows = s_rows.shape[0]
    return pl.pallas_call(
        _ada_kernel,
        grid=(depth, n // tn),
        in_specs=[pl.BlockSpec((rows, d), lambda i, j: (0, 0)),
                  pl.BlockSpec((None, d, tn), lambda i, j: (i, 0, j)),
                  pl.BlockSpec((None, 1, tn), lambda i, j: (i, 0, j))],
        out_specs=pl.BlockSpec((None, rows, tn), lambda i, j: (i, 0, j)),
        out_shape=jax.ShapeDtypeStruct((depth, rows, n), F32),
        compiler_params=_params(2),
        name="ada_params",
    )(s_rows, w_ada, b_ada.reshape(depth, 1, n))


def _qkv_kernel(x_ref, sc_ref, sh_ref, w_ref, b_ref, q_ref, k_ref, v_ref, *, nc):
    d = x_ref.shape[-1]
    h = _modulate(x_ref[...], sc_ref[...], sh_ref[...])
    for idx, o_ref in enumerate((q_ref, k_ref, v_ref)):
        for n in range(d // nc):
            col = idx * d + n * nc
            y = _dot(h, w_ref[:, col:col + nc]) + b_ref[:, col:col + nc]
            if idx == 0:
                y = y * (HEAD_DIM ** -0.5 * LOG2E)
            o_ref[:, n * nc:(n + 1) * nc] = y.astype(BF16)


def _qkv(x, ada, layer, ctx_row, w, b, slot, tm):
    bsz, l, d = x.shape
    tok = pl.BlockSpec((None, tm, d), lambda i, j: (i, j, 0))
    out = jax.ShapeDtypeStruct((bsz, l, d), BF16)
    return pl.pallas_call(
        functools.partial(_qkv_kernel, nc=512),
        grid=(bsz, l // tm),
        in_specs=[tok, _mod_block(layer, 1, d, ctx_row), _mod_block(layer, 0, d, ctx_row),
                  _layer_block(slot, (d, 3 * d)), _layer_block(slot, (1, 3 * d))],
        out_specs=[tok, tok, tok],
        out_shape=[out, out, out],
        compiler_params=_params(2),
        name="modulate_qkv",
    )(x, ada, ada, w, b)


def _bias_patterns(rows):
    last = rows - Q_ROWS
    return ((0, 0), (Q_ROWS, Q_ROWS - WIN_H // 2), (last, rows - KEY_ROWS))


def _bias_kernel(t_ref, o_ref, *, rows):
    for pat, (qrow0, w0) in enumerate(_bias_patterns(rows)):
        for i in range(Q_ROWS):
            r = qrow0 + i
            r0 = min(max(r - WIN_H // 2, 0), rows - WIN_H)
            for j in range(KEY_ROWS):
                kr = w0 + j
                lane0 = (j % 2) * GRID_W
                dst = (pat, slice(i * GRID_W, (i + 1) * GRID_W), slice(j * GRID_W, (j + 1) * GRID_W))
                if r0 <= kr < r0 + WIN_H:
                    o_ref[dst] = t_ref[kr - r + WIN_H - 1, :, lane0:lane0 + GRID_W] * LOG2E
                else:
                    o_ref[dst] = jnp.full((GRID_W, GRID_W), NEG_INF, F32)


def _bias_tables(rpb, rows):
    n_layers, nh, nd, no = rpb.shape
    qc = np.arange(GRID_W)[:, None]
    kc = np.arange(GRID_W)[None, :]
    qstart = np.clip(qc - WIN_W // 2, 0, GRID_W - WIN_W)
    col_ok = (kc >= qstart) & (kc < qstart + WIN_W)
    off = kc - qc + WIN_W - 1
    onehot = jnp.asarray(((off[None] == np.arange(no)[:, None, None]) & col_ok[None]).astype(np.float32))
    toep = jnp.einsum('lhdo,oqk->lhdqk', rpb, onehot, precision=lax.Precision.HIGHEST)
    toep = jnp.where(jnp.asarray(col_ok), toep, NEG_INF)
    toep = jnp.concatenate([toep, toep], axis=-1)
    tq, tk = Q_ROWS * GRID_W, KEY_ROWS * GRID_W
    return pl.pallas_call(
        functools.partial(_bias_kernel, rows=rows),
        grid=(n_layers, nh),
        in_specs=[pl.BlockSpec((None, None, nd, GRID_W, 2 * GRID_W), lambda a, h: (a, h, 0, 0, 0))],
        out_specs=pl.BlockSpec((None, 3, None, tq, tk), lambda a, h: (a, 0, h, 0, 0)),
        out_shape=jax.ShapeDtypeStruct((n_layers, 3, nh, tq, tk), F32),
        compiler_params=_params(2),
        name="attn_bias_table",
    )(toep)


def _nattn_kernel(q_ref, k_ref, v_ref, kc_ref, vc_ref, bias_ref, o_ref, *, rows):
    gs = pl.program_id(2)
    n_steps = pl.num_programs(2)
    lanes = q_ref.shape[-1]
    tq = Q_ROWS * GRID_W
    n_keys = KEY_ROWS * GRID_W
    n_ctx = kc_ref.shape[0]
    lane = lax.broadcasted_iota(jnp.int32, (1, lanes), 1)
    kc = kc_ref[...]
    vc = vc_ref[...]
    for sg in range(GROUPS_PER_STEP):
        g = gs * GROUPS_PER_STEP + sg
        if sg == 0:
            pat = jnp.where(gs == 0, 0, 1)
        elif sg == GROUPS_PER_STEP - 1:
            pat = jnp.where(gs == n_steps - 1, 2, 1)
        else:
            pat = 1
        w0 = jnp.clip(g * Q_ROWS - WIN_H // 2, 0, rows - KEY_ROWS)
        start = pl.multiple_of(w0 * GRID_W, (WIN_H // 2) * GRID_W)
        q = q_ref[sg * tq:(sg + 1) * tq, :]
        kall = jnp.concatenate([kc, k_ref[pl.ds(start, n_keys), :]], axis=0)
        vall = jnp.concatenate([vc, v_ref[pl.ds(start, n_keys), :]], axis=0)
        vext = jnp.concatenate([vall, jnp.ones_like(vall)], axis=1)
        qq = jnp.concatenate([jnp.where((lane // HEAD_DIM) == h, q, jnp.zeros_like(q))
                              for h in range(HEADS_PER_STEP)], axis=0)
        bias = bias_ref[pat].reshape(HEADS_PER_STEP * tq, n_keys)
        m = o = None
        for lo in range(0, n_ctx + n_keys, KEY_TILE):
            hi = lo + KEY_TILE
            st = _dot_nt(qq, kall[lo:hi])
            b_lo = max(lo, n_ctx)
            if b_lo == lo:
                st = st + bias[:, lo - n_ctx:hi - n_ctx]
            elif b_lo < hi:
                st = jnp.concatenate([st[:, :b_lo - lo], st[:, b_lo - lo:] + bias[:, :hi - n_ctx]], axis=1)
            m_new = jnp.max(st, axis=-1, keepdims=True)
            if m is not None:
                m_new = jnp.maximum(m, m_new)
            ot = _dot(jnp.exp2(st - m_new).astype(BF16), vext[lo:hi])
            o = ot if o is None else o * jnp.exp2(m - m_new) + ot
            m = m_new
        o = o[:, :lanes] / o[:, lanes:]
        o_ref[sg * tq:(sg + 1) * tq, :] = jnp.where(lane < HEAD_DIM, o[:tq], o[tq:]).astype(BF16)


def _nattn(q, k, v, kc, vc, bias, slot):
    bsz, n, d = q.shape
    cl = kc.shape[1]
    rows = n // GRID_W
    lanes = HEADS_PER_STEP * HEAD_DIM
    tq = Q_ROWS * GRID_W
    step_rows = GROUPS_PER_STEP * Q_ROWS
    qspec = pl.BlockSpec((None, step_rows * GRID_W, lanes), lambda hp, b, g: (b, g, hp))
    kvspec = pl.BlockSpec((None, n, lanes), lambda hp, b, g: (b, 0, hp))
    cspec = pl.BlockSpec((None, cl, lanes), lambda hp, b, g: (b, 0, hp))
    bspec = pl.BlockSpec((None, 3, HEADS_PER_STEP, tq, KEY_ROWS * GRID_W), lambda hp, b, g: (slot, 0, hp, 0, 0),
                         pipeline_mode=pl.Buffered(1))
    return pl.pallas_call(
        functools.partial(_nattn_kernel, rows=rows),
        grid=(d // lanes, bsz, rows // step_rows),
        in_specs=[qspec, kvspec, kvspec, cspec, cspec, bspec],
        out_specs=qspec,
        out_shape=jax.ShapeDtypeStruct((bsz, n, d), BF16),
        compiler_params=_params(3),
        name="neighborhood_attention",
    )(q, k, v, kc, vc, bias)


def _cattn_kernel(q_ref, k_ref, v_ref, o_ref):
    lanes = HEADS_PER_STEP * HEAD_DIM
    for hp in range(q_ref.shape[-1] // lanes):
        outs = []
        for h in range(HEADS_PER_STEP):
            sl = slice(hp * lanes + h * HEAD_DIM, hp * lanes + (h + 1) * HEAD_DIM)
            s = _dot_nt(q_ref[:, sl], k_ref[:, sl])
            p = jnp.exp2(s - jnp.max(s, axis=-1, keepdims=True))
            denom = jnp.sum(p, axis=-1, keepdims=True)
            outs.append(_dot(p.astype(BF16), v_ref[:, sl]) / denom)
        o_ref[:, hp * lanes:(hp + 1) * lanes] = jnp.concatenate(outs, axis=-1).astype(BF16)


def _cattn(q, k, v):
    bsz, cl, d = q.shape
    spec = pl.BlockSpec((None, cl, d), lambda b: (b, 0, 0))
    return pl.pallas_call(
        _cattn_kernel,
        grid=(bsz,),
        in_specs=[spec, spec, spec],
        out_specs=spec,
        out_shape=jax.ShapeDtypeStruct((bsz, cl, d), BF16),
        compiler_params=_params(1),
        name="context_attention",
    )(q, k, v)


def _ffn_chunks(d_ff, chunk=512):
    edges = list(range(0, d_ff, chunk)) + [d_ff]
    return list(zip(edges[:-1], edges[1:]))


def _ffn_kernel(*refs, alpha, with_proj):
    if with_proj:
        o_ref, wo_ref, bo_ref, g1_ref, lmg_ref, lmb_ref, *refs = refs
    x_ref, sc_ref, sh_ref, g_ref, w1_ref, w3_ref, w2_ref, lg_ref, lb_ref, out_ref = refs
    x = x_ref[...]
    if with_proj:
        y = _dot(o_ref[...], wo_ref[...]) + bo_ref[...]
        x = _layer_norm(alpha * x + g1_ref[...] * y, lmg_ref[...], lmb_ref[...])
    h = _modulate(x, sc_ref[...], sh_ref[...])
    acc = None
    for lo, hi in _ffn_chunks(w1_ref.shape[1]):
        a = _dot(h, w1_ref[:, lo:hi])
        b = _dot(h, w3_ref[:, lo:hi])
        act = (a * jax.nn.sigmoid(a) * b).astype(BF16)
        part = _dot(act, w2_ref[lo:hi, :])
        acc = part if acc is None else acc + part
    z = alpha * x + g_ref[...] * acc
    out_ref[...] = _layer_norm(z, lg_ref[...], lb_ref[...])


def _ffn(x, ada, layer, ctx_row, w1, w3, w2, ln_g, ln_b, alpha, tm, proj=None):
    bsz, l, d = x.shape
    d_ff = w1.shape[-1]
    tok = pl.BlockSpec((None, tm, d), lambda i, j: (i, j, 0))
    lrow = _layer_block(layer, (1, d))
    specs = [tok, _mod_block(layer, 4, d, ctx_row), _mod_block(layer, 3, d, ctx_row),
             _mod_block(layer, 5, d, ctx_row),
             _layer_block(layer, (d, d_ff)), _layer_block(layer, (d, d_ff)), _layer_block(layer, (d_ff, d)),
             lrow, lrow]
    args = [x, ada, ada, ada, w1, w3, w2, ln_g, ln_b]
    if proj is not None:
        o, wo, bo, slot, lmg, lmb = proj
        specs = [tok, _layer_block(slot, (d, d)), _layer_block(slot, (1, d)),
                 _mod_block(layer, 2, d, ctx_row), lrow, lrow] + specs
        args = [o, wo, bo, ada, lmg, lmb] + args
    return pl.pallas_call(
        functools.partial(_ffn_kernel, alpha=alpha, with_proj=proj is not None),
        grid=(bsz, l // tm),
        in_specs=specs,
        out_specs=tok,
        out_shape=jax.ShapeDtypeStruct((bsz, l, d), F32),
        compiler_params=_params(2),
        name="outproj_swiglu_deepnorm" if proj is not None else "swiglu_deepnorm",
    )(*args)


def _glu_kernel(x_ref, sc_ref, sh_ref, w_ref, b_ref, u_ref, *, nc):
    d = x_ref.shape[-1]
    h = _modulate(x_ref[...], sc_ref[...], sh_ref[...])
    for n in range(d // nc):
        lo = n * nc
        a = _dot(h, w_ref[:, lo:lo + nc]) + b_ref[:, lo:lo + nc]
        g = _dot(h, w_ref[:, d + lo:d + lo + nc]) + b_ref[:, d + lo:d + lo + nc]
        u_ref[:, lo:lo + nc] = a * jax.nn.sigmoid(g)


def _glu(x, ada, layer, ctx_row, w, b, slot, tm):
    bsz, l, d = x.shape
    tok = pl.BlockSpec((None, tm, d), lambda i, j: (i, j, 0))
    return pl.pallas_call(
        functools.partial(_glu_kernel, nc=512),
        grid=(bsz, l // tm),
        in_specs=[tok, _mod_block(layer, 1, d, ctx_row), _mod_block(layer, 0, d, ctx_row),
                  _layer_block(slot, (d, 2 * d)), _layer_block(slot, (1, 2 * d))],
        out_specs=tok,
        out_shape=jax.ShapeDtypeStruct((bsz, l, d), F32),
        compiler_params=_params(2),
        name="modulate_pw1_glu",
    )(x, ada, ada, w, b)


def _conv_kernel(up_ref, u_ref, un_ref, wdw_ref, bdw_ref, cg_ref, cb_ref, w2_ref, b2_ref,
                 x_ref, g_ref, lg_ref, lb_ref, out_ref, ext_ref, y_ref, *, alpha, rc, lc):
    j = pl.program_id(1)
    tm, d = u_ref.shape
    pad = CONV_K // 2
    ext_ref[0:HALO, :] = jnp.where(j == 0, 0.0, up_ref[...])
    ext_ref[HALO:HALO + tm, :] = u_ref[...]
    ext_ref[HALO + tm:HALO + tm + HALO, :] = jnp.where(j == pl.num_programs(1) - 1, 0.0, un_ref[...])

    span = HALO - pad + CONV_K - 1
    span8 = -(-span // 8) * 8

    def row_chunk(c, carry):
        r0 = pl.multiple_of(c * rc, rc)
        for l0 in range(0, d, lc):
            win = ext_ref[pl.ds(r0, rc + span8), l0:l0 + lc]
            acc = jnp.zeros((rc // 8, 8, lc), F32)
            for r in range(8):
                shifted = win if r == 0 else pltpu.roll(win, rc + span8 - r, axis=0)
                shifted = shifted.reshape((rc + span8) // 8, 8, lc)
                for a in range(span8 // 8):
                    k = 8 * a + r - (HALO - pad)
                    if 0 <= k < CONV_K:
                        acc = acc + shifted[a:a + rc // 8] * wdw_ref[k, :, l0:l0 + lc][None]
            y_ref[pl.ds(r0, rc), l0:l0 + lc] = acc.reshape(rc, lc)
        return carry

    lax.fori_loop(0, tm // rc, row_chunk, 0)
    t = _layer_norm(y_ref[...] + bdw_ref[...], cg_ref[...], cb_ref[...])
    t = (t * jax.nn.sigmoid(t)).astype(BF16)
    y = _dot(t, w2_ref[...]) + b2_ref[...]
    z = alpha * x_ref[...] + g_ref[...] * y
    out_ref[...] = _layer_norm(z, lg_ref[...], lb_ref[...])


def _conv(u, w_dw8, b_dw, cg, cb, w2, b2, slot, x, ada, layer, ctx_row, ln_g, ln_b, alpha, tm):
    bsz, l, d = x.shape
    nh = tm // HALO
    last = l // HALO - 1
    tok = pl.BlockSpec((None, tm, d), lambda i, j: (i, j, 0))
    prev = pl.BlockSpec((None, HALO, d), lambda i, j: (i, jnp.maximum(j * nh - 1, 0), 0))
    nxt = pl.BlockSpec((None, HALO, d), lambda i, j: (i, jnp.minimum((j + 1) * nh, last), 0))
    srow = _layer_block(slot, (1, d))
    lrow = _layer_block(layer, (1, d))
    return pl.pallas_call(
        functools.partial(_conv_kernel, alpha=alpha, rc=64, lc=128),
        grid=(bsz, l // tm),
        in_specs=[prev, tok, nxt, _layer_block(slot, (CONV_K, 8, d)), srow, srow, srow,
                  _layer_block(slot, (d, d)), srow, tok, _mod_block(layer, 2, d, ctx_row), lrow, lrow],
        out_specs=tok,
        out_shape=jax.ShapeDtypeStruct((bsz, l, d), F32),
        scratch_shapes=[pltpu.VMEM((tm + 2 * HALO, d), F32), pltpu.VMEM((tm, d), F32)],
        compiler_params=_params(2),
        name="dwconv_pw2_deepnorm",
    )(u, u, u, w_dw8, b_dw, cg, cb, w2, b2, x, ada, ln_g, ln_b)


def kernel(x, c, ctx, c_ctx, w_ada, b_ada, ln_mix_g, ln_mix_b, ln_ffn_g, ln_ffn_b, attn_w_qkv, attn_b_qkv, attn_w_o, attn_b_o, attn_rpb, conv_w_pw1, conv_b_pw1, conv_w_dw, conv_b_dw, conv_ln_g, conv_ln_b, conv_w_pw2, conv_b_pw2, ffn_w1, ffn_w3, ffn_w2):
    bsz, n, d = x.shape
    cl = ctx.shape[1]
    depth = w_ada.shape[0]
    rows = n // GRID_W
    assert d == N_HEADS * HEAD_DIM and HEADS_PER_STEP == 2 and bsz < ADA_ROWS
    assert n % (GROUPS_PER_STEP * Q_ROWS * GRID_W) == 0 and rows >= 2 * KEY_ROWS
    assert cl <= KEY_TILE and (cl + KEY_ROWS * GRID_W) % KEY_TILE == 0
    alpha = (2 * depth) ** 0.25
    last_attn = max(i for i in range(depth) if i % N_MIXERS == 0)
    tm = min(512, n)
    tmc = min(512, cl)
    ctx_row = bsz

    s_rows = jnp.zeros((ADA_ROWS, d), F32).at[:bsz].set(c).at[ctx_row].set(c_ctx)
    ada = _ada_all(s_rows, w_ada, b_ada).reshape(depth, ADA_ROWS, 1, N_MODS * d)

    bf = lambda w: w.astype(BF16)
    rows3 = lambda v: v.reshape(v.shape[0], 1, v.shape[1])
    w1, w3, w2 = bf(ffn_w1), bf(ffn_w3), bf(ffn_w2)
    wqkv, wo, wp1, wp2 = bf(attn_w_qkv), bf(attn_w_o), bf(conv_w_pw1), bf(conv_w_pw2)
    bqkv, bo, bp1, bp2 = rows3(attn_b_qkv), rows3(attn_b_o), rows3(conv_b_pw1), rows3(conv_b_pw2)
    lmg, lmb, lfg, lfb = rows3(ln_mix_g), rows3(ln_mix_b), rows3(ln_ffn_g), rows3(ln_ffn_b)
    conv_args = (jnp.broadcast_to(conv_w_dw[:, :, None, :], conv_w_dw.shape[:2] + (8, d)),
                 rows3(conv_b_dw), rows3(conv_ln_g), rows3(conv_ln_b), wp2, bp2)
    bias = _bias_tables(attn_rpb, rows)

    xc = ctx
    for i in range(depth):
        slot = i // N_MIXERS
        ctx_live = i < last_attn
        if i % N_MIXERS == 0:
            q, k, v = _qkv(x, ada, i, None, wqkv, bqkv, slot, tm)
            qc, kc, vc = _qkv(xc, ada, i, ctx_row, wqkv, bqkv, slot, tmc)
            proj = (_nattn(q, k, v, kc, vc, bias, slot), wo, bo, slot, lmg, lmb)
            proj_c = (_cattn(qc, kc, vc), wo, bo, slot, lmg, lmb) if ctx_live else None
        else:
            proj = proj_c = None
            u = _glu(x, ada, i, None, wp1, bp1, slot, tm)
            x = _conv(u, *conv_args, slot, x, ada, i, None, lmg, lmb, alpha, tm)
            if ctx_live:
                uc = _glu(xc, ada, i, ctx_row, wp1, bp1, slot, tmc)
                xc = _conv(uc, *conv_args, slot, xc, ada, i, ctx_row, lmg, lmb, alpha, tmc)
        x = _ffn(x, ada, i, None, w1, w3, w2, lfg, lfb, alpha, tm, proj)
        if ctx_live:
            xc = _ffn(xc, ada, i, ctx_row, w1, w3, w2, lfg, lfb, alpha, tmc, proj_c)
    return x
```

```python
import functools

import numpy as np
import jax
import jax.numpy as jnp
from jax import lax
from jax.experimental import pallas as pl
from jax.experimental.pallas import tpu as pltpu

F32 = jnp.float32
BF16 = jnp.bfloat16

N_HEADS = 16
HEAD_DIM = 64
GRID_W = 64
WIN_H = 8
WIN_W = 16
CONV_K = 31
N_MIXERS = 2
N_MODS = 6
LN_EPS = 1e-5
NEG_INF = -1e30

LOG2E = 1.4426950408889634
Q_ROWS = 4
KEY_ROWS = Q_ROWS + WIN_H
GROUPS_PER_STEP = 16
KEY_TILE = 512
HEADS_PER_STEP = 2
HALO = 16
ADA_ROWS = 16
VMEM_LIMIT = 56 * 1024 * 1024


def _params(n_axes):
    return pltpu.CompilerParams(dimension_semantics=("arbitrary",) * n_axes,
                                vmem_limit_bytes=VMEM_LIMIT)


def _layer_block(layer, shape):
    return pl.BlockSpec((None,) + tuple(shape), lambda *_: (layer,) + (0,) * len(shape),
                        pipeline_mode=pl.Buffered(1))


def _mod_block(layer, chunk, d, ctx_row):
    if ctx_row is None:
        return pl.BlockSpec((None, None, 1, d), lambda i, j: (layer, i, 0, chunk))
    return pl.BlockSpec((None, None, 1, d), lambda i, j: (layer, ctx_row, 0, chunk))


def _layer_norm(z, g, b):
    mu = jnp.mean(z, axis=-1, keepdims=True)
    zc = z - mu
    var = jnp.mean(zc * zc, axis=-1, keepdims=True)
    return zc * lax.rsqrt(var + LN_EPS) * g + b


def _modulate(x, sc, sh):
    return (x * (1.0 + sc) + sh).astype(BF16)


def _dot(a, b):
    return jnp.dot(a, b, preferred_element_type=F32)


def _dot_nt(a, b):
    return lax.dot_general(a, b, (((1,), (1,)), ((), ())), preferred_element_type=F32)


def _ada_kernel(s_ref, w_ref, b_ref, o_ref):
    s = s_ref[...]
    s = s * jax.nn.sigmoid(s)
    o_ref[...] = _dot(s.astype(BF16), w_ref[...].astype(BF16)) + b_ref[...]


def _ada_all(s_rows, w_ada, b_ada, tn=1536):
    depth, d, n = w_ada.shape
    rows = s_rows.shape[0]
    return pl.pallas_call(
        _ada_kernel,
        grid=(depth, n // tn),
        in_specs=[pl.BlockSpec((rows, d), lambda i, j: (0, 0)),
                  pl.BlockSpec((None, d, tn), lambda i, j: (i, 0, j)),
                  pl.BlockSpec((None, 1, tn), lambda i, j: (i, 0, j))],
        out_specs=pl.BlockSpec((None, rows, tn), lambda i, j: (i, 0, j)),
        out_shape=jax.ShapeDtypeStruct((depth, rows, n), F32),
        compiler_params=_params(2),
        name="ada_params",
    )(s_rows, w_ada, b_ada.reshape(depth, 1, n))


def _qkv_kernel(x_ref, sc_ref, sh_ref, w_ref, b_ref, q_ref, k_ref, v_ref, *, nc):
    d = x_ref.shape[-1]
    h = _modulate(x_ref[...], sc_ref[...], sh_ref[...])
    for idx, o_ref in enumerate((q_ref, k_ref, v_ref)):
        for n in range(d // nc):
            col = idx * d + n * nc
            y = _dot(h, w_ref[:, col:col + nc]) + b_ref[:, col:col + nc]
            if idx == 0:
                y = y * (HEAD_DIM ** -0.5 * LOG2E)
            o_ref[:, n * nc:(n + 1) * nc] = y.astype(BF16)


def _qkv(x, ada, layer, ctx_row, w, b, slot, tm):
    bsz, l, d = x.shape
    tok = pl.BlockSpec((None, tm, d), lambda i, j: (i, j, 0))
    out = jax.ShapeDtypeStruct((bsz, l, d), BF16)
    return pl.pallas_call(
        functools.partial(_qkv_kernel, nc=512),
        grid=(bsz, l // tm),
        in_specs=[tok, _mod_block(layer, 1, d, ctx_row), _mod_block(layer, 0, d, ctx_row),
                  _layer_block(slot, (d, 3 * d)), _layer_block(slot, (1, 3 * d))],
        out_specs=[tok, tok, tok],
        out_shape=[out, out, out],
        compiler_params=_params(2),
        name="modulate_qkv",
    )(x, ada, ada, w, b)


def _bias_patterns(rows):
    last = rows - Q_ROWS
    return ((0, 0), (Q_ROWS, Q_ROWS - WIN_H // 2), (last, rows - KEY_ROWS))


def _bias_kernel(t_ref, o_ref, *, rows):
    for pat, (qrow0, w0) in enumerate(_bias_patterns(rows)):
        for i in range(Q_ROWS):
            r = qrow0 + i
            r0 = min(max(r - WIN_H // 2, 0), rows - WIN_H)
            for j in range(KEY_ROWS):
                kr = w0 + j
                lane0 = (j % 2) * GRID_W
                dst = (pat, slice(i * GRID_W, (i + 1) * GRID_W), slice(j * GRID_W, (j + 1) * GRID_W))
                if r0 <= kr < r0 + WIN_H:
                    o_ref[dst] = t_ref[kr - r + WIN_H - 1, :, lane0:lane0 + GRID_W] * LOG2E
                else:
                    o_ref[dst] = jnp.full((GRID_W, GRID_W), NEG_INF, F32)


def _bias_tables(rpb, rows):
    n_layers, nh, nd, no = rpb.shape
    qc = np.arange(GRID_W)[:, None]
    kc = np.arange(GRID_W)[None, :]
    qstart = np.clip(qc - WIN_W // 2, 0, GRID_W - WIN_W)
    col_ok = (kc >= qstart) & (kc < qstart + WIN_W)
    off = kc - qc + WIN_W - 1
    onehot = jnp.asarray(((off[None] == np.arange(no)[:, None, None]) & col_ok[None]).astype(np.float32))
    toep = jnp.einsum('lhdo,oqk->lhdqk', rpb, onehot, precision=lax.Precision.HIGHEST)
    toep = jnp.where(jnp.asarray(col_ok), toep, NEG_INF)
    toep = jnp.concatenate([toep, toep], axis=-1)
    tq, tk = Q_ROWS * GRID_W, KEY_ROWS * GRID_W
    return pl.pallas_call(
        functools.partial(_bias_kernel, rows=rows),
        grid=(n_layers, nh),
        in_specs=[pl.BlockSpec((None, None, nd, GRID_W, 2 * GRID_W), lambda a, h: (a, h, 0, 0, 0))],
        out_specs=pl.BlockSpec((None, 3, None, tq, tk), lambda a, h: (a, 0, h, 0, 0)),
        out_shape=jax.ShapeDtypeStruct((n_layers, 3, nh, tq, tk), F32),
        compiler_params=_params(2),
        name="attn_bias_table",
    )(toep)


def _nattn_kernel(q_ref, k_ref, v_ref, kc_ref, vc_ref, bias_ref, o_ref, *, rows):
    gs = pl.program_id(2)
    n_steps = pl.num_programs(2)
    lanes = q_ref.shape[-1]
    tq = Q_ROWS * GRID_W
    n_keys = KEY_ROWS * GRID_W
    n_ctx = kc_ref.shape[0]
    lane = lax.broadcasted_iota(jnp.int32, (1, lanes), 1)
    kc = kc_ref[...]
    vc = vc_ref[...]
    for sg in range(GROUPS_PER_STEP):
        g = gs * GROUPS_PER_STEP + sg
        if sg == 0:
            pat = jnp.where(gs == 0, 0, 1)
        elif sg == GROUPS_PER_STEP - 1:
            pat = jnp.where(gs == n_steps - 1, 2, 1)
        else:
            pat = 1
        w0 = jnp.clip(g * Q_ROWS - WIN_H // 2, 0, rows - KEY_ROWS)
        start = pl.multiple_of(w0 * GRID_W, (WIN_H // 2) * GRID_W)
        q = q_ref[sg * tq:(sg + 1) * tq, :]
        kall = jnp.concatenate([kc, k_ref[pl.ds(start, n_keys), :]], axis=0)
        vall = jnp.concatenate([vc, v_ref[pl.ds(start, n_keys), :]], axis=0)
        vext = jnp.concatenate([vall, jnp.ones_like(vall)], axis=1)
        qq = jnp.concatenate([jnp.where((lane // HEAD_DIM) == h, q, jnp.zeros_like(q))
                              for h in range(HEADS_PER_STEP)], axis=0)
        bias = bias_ref[pat].reshape(HEADS_PER_STEP * tq, n_keys)
        m = o = None
        for lo in range(0, n_ctx + n_keys, KEY_TILE):
            hi = lo + KEY_TILE
            st = _dot_nt(qq, kall[lo:hi])
            b_lo = max(lo, n_ctx)
            if b_lo == lo:
                st = st + bias[:, lo - n_ctx:hi - n_ctx]
            elif b_lo < hi:
                st = jnp.concatenate([st[:, :b_lo - lo], st[:, b_lo - lo:] + bias[:, :hi - n_ctx]], axis=1)
            m_new = jnp.max(st, axis=-1, keepdims=True)
            if m is not None:
                m_new = jnp.maximum(m, m_new)
            ot = _dot(jnp.exp2(st - m_new).astype(BF16), vext[lo:hi])
            o = ot if o is None else o * jnp.exp2(m - m_new) + ot
            m = m_new
        o = o[:, :lanes] / o[:, lanes:]
        o_ref[sg * tq:(sg + 1) * tq, :] = jnp.where(lane < HEAD_DIM, o[:tq], o[tq:]).astype(BF16)


def _nattn(q, k, v, kc, vc, bias, slot):
    bsz, n, d = q.shape
    cl = kc.shape[1]
    rows = n // GRID_W
    lanes = HEADS_PER_STEP * HEAD_DIM
    tq = Q_ROWS * GRID_W
    step_rows = GROUPS_PER_STEP * Q_ROWS
    qspec = pl.BlockSpec((None, step_rows * GRID_W, lanes), lambda hp, b, g: (b, g, hp))
    kvspec = pl.BlockSpec((None, n, lanes), lambda hp, b, g: (b, 0, hp))
    cspec = pl.BlockSpec((None, cl, lanes), lambda hp, b, g: (b, 0, hp))
    bspec = pl.BlockSpec((None, 3, HEADS_PER_STEP, tq, KEY_ROWS * GRID_W), lambda hp, b, g: (slot, 0, hp, 0, 0),
                         pipeline_mode=pl.Buffered(1))
    return pl.pallas_call(
        functools.partial(_nattn_kernel, rows=rows),
        grid=(d // lanes, bsz, rows // step_rows),
        in_specs=[qspec, kvspec, kvspec, cspec, cspec, bspec],
        out_specs=qspec,
        out_shape=jax.ShapeDtypeStruct((bsz, n, d), BF16),
        compiler_params=_params(3),
        name="neighborhood_attention",
    )(q, k, v, kc, vc, bias)


def _cattn_kernel(q_ref, k_ref, v_ref, o_ref):
    lanes = HEADS_PER_STEP * HEAD_DIM
    for hp in range(q_ref.shape[-1] // lanes):
        outs = []
        for h in range(HEADS_PER_STEP):
            sl = slice(hp * lanes + h * HEAD_DIM, hp * lanes + (h + 1) * HEAD_DIM)
            s = _dot_nt(q_ref[:, sl], k_ref[:, sl])
            p = jnp.exp2(s - jnp.max(s, axis=-1, keepdims=True))
            denom = jnp.sum(p, axis=-1, keepdims=True)
            outs.append(_dot(p.astype(BF16), v_ref[:, sl]) / denom)
        o_ref[:, hp * lanes:(hp + 1) * lanes] = jnp.concatenate(outs, axis=-1).astype(BF16)


def _cattn(q, k, v):
    bsz, cl, d = q.shape
    spec = pl.BlockSpec((None, cl, d), lambda b: (b, 0, 0))
    return pl.pallas_call(
        _cattn_kernel,
        grid=(bsz,),
        in_specs=[spec, spec, spec],
        out_specs=spec,
        out_shape=jax.ShapeDtypeStruct((bsz, cl, d), BF16),
        compiler_params=_params(1),
        name="context_attention",
    )(q, k, v)


def _ffn_chunks(d_ff, chunk=512):
    edges = list(range(0, d_ff, chunk)) + [d_ff]
    return list(zip(edges[:-1], edges[1:]))


def _ffn_kernel(*refs, alpha, with_proj):
    if with_proj:
        o_ref, wo_ref, bo_ref, g1_ref, lmg_ref, lmb_ref, *refs = refs
    x_ref, sc_ref, sh_ref, g_ref, w1_ref, w3_ref, w2_ref, lg_ref, lb_ref, out_ref = refs
    x = x_ref[...]
    if with_proj:
        y = _dot(o_ref[...], wo_ref[...]) + bo_ref[...]
        x = _layer_norm(alpha * x + g1_ref[...] * y, lmg_ref[...], lmb_ref[...])
    h = _modulate(x, sc_ref[...], sh_ref[...])
    acc = None
    for lo, hi in _ffn_chunks(w1_ref.shape[1]):
        a = _dot(h, w1_ref[:, lo:hi])
        b = _dot(h, w3_ref[:, lo:hi])
        act = (a * jax.nn.sigmoid(a) * b).astype(BF16)
        part = _dot(act, w2_ref[lo:hi, :])
        acc = part if acc is None else acc + part
    z = alpha * x + g_ref[...] * acc
    out_ref[...] = _layer_norm(z, lg_ref[...], lb_ref[...])


def _ffn(x, ada, layer, ctx_row, w1, w3, w2, ln_g, ln_b, alpha, tm, proj=None):
    bsz, l, d = x.shape
    d_ff = w1.shape[-1]
    tok = pl.BlockSpec((None, tm, d), lambda i, j: (i, j, 0))
    lrow = _layer_block(layer, (1, d))
    specs = [tok, _mod_block(layer, 4, d, ctx_row), _mod_block(layer, 3, d, ctx_row),
             _mod_block(layer, 5, d, ctx_row),
             _layer_block(layer, (d, d_ff)), _layer_block(layer, (d, d_ff)), _layer_block(layer, (d_ff, d)),
             lrow, lrow]
    args = [x, ada, ada, ada, w1, w3, w2, ln_g, ln_b]
    if proj is not None:
        o, wo, bo, slot, lmg, lmb = proj
        specs = [tok, _layer_block(slot, (d, d)), _layer_block(slot, (1, d)),
                 _mod_block(layer, 2, d, ctx_row), lrow, lrow] + specs
        args = [o, wo, bo, ada, lmg, lmb] + args
    return pl.pallas_call(
        functools.partial(_ffn_kernel, alpha=alpha, with_proj=proj is not None),
        grid=(bsz, l // tm),
        in_specs=specs,
        out_specs=tok,
        out_shape=jax.ShapeDtypeStruct((bsz, l, d), F32),
        compiler_params=_params(2),
        name="outproj_swiglu_deepnorm" if proj is not None else "swiglu_deepnorm",
    )(*args)


def _glu_kernel(x_ref, sc_ref, sh_ref, w_ref, b_ref, u_ref, *, nc):
    d = x_ref.shape[-1]
    h = _modulate(x_ref[...], sc_ref[...], sh_ref[...])
    for n in range(d // nc):
        lo = n * nc
        a = _dot(h, w_ref[:, lo:lo + nc]) + b_ref[:, lo:lo + nc]
        g = _dot(h, w_ref[:, d + lo:d + lo + nc]) + b_ref[:, d + lo:d + lo + nc]
        u_ref[:, lo:lo + nc] = a * jax.nn.sigmoid(g)


def _glu(x, ada, layer, ctx_row, w, b, slot, tm):
    bsz, l, d = x.shape
    tok = pl.BlockSpec((None, tm, d), lambda i, j: (i, j, 0))
    return pl.pallas_call(
        functools.partial(_glu_kernel, nc=256),
        grid=(bsz, l // tm),
        in_specs=[tok, _mod_block(layer, 1, d, ctx_row), _mod_block(layer, 0, d, ctx_row),
                  _layer_block(slot, (d, 2 * d)), _layer_block(slot, (1, 2 * d))],
        out_specs=tok,
        out_shape=jax.ShapeDtypeStruct((bsz, l, d), F32),
        compiler_params=_params(2),
        name="modulate_pw1_glu",
    )(x, ada, ada, w, b)


def _conv_kernel(up_ref, u_ref, un_ref, wdw_ref, bdw_ref, cg_ref, cb_ref, w2_ref, b2_ref,
                 x_ref, g_ref, lg_ref, lb_ref, out_ref, ext_ref, y_ref, *, alpha, rc, lc):
    j = pl.program_id(1)
    tm, d = u_ref.shape
    pad = CONV_K // 2
    ext_ref[0:HALO, :] = jnp.where(j == 0, 0.0, up_ref[...])
    ext_ref[HALO:HALO + tm, :] = u_ref[...]
    ext_ref[HALO + tm:HALO + tm + HALO, :] = jnp.where(j == pl.num_programs(1) - 1, 0.0, un_ref[...])

    span = HALO - pad + CONV_K - 1
    span8 = -(-span // 8) * 8

    def row_chunk(c, carry):
        r0 = pl.multiple_of(c * rc, rc)
        for l0 in range(0, d, lc):
            win = ext_ref[pl.ds(r0, rc + span8), l0:l0 + lc]
            acc = jnp.zeros((rc // 8, 8, lc), F32)
            for r in range(8):
                shifted = win if r == 0 else pltpu.roll(win, rc + span8 - r, axis=0)
                shifted = shifted.reshape((rc + span8) // 8, 8, lc)
                for a in range(span8 // 8):
                    k = 8 * a + r - (HALO - pad)
                    if 0 <= k < CONV_K:
                        acc = acc + shifted[a:a + rc // 8] * wdw_ref[k, :, l0:l0 + lc][None]
            y_ref[pl.ds(r0, rc), l0:l0 + lc] = acc.reshape(rc, lc)
        return carry

    lax.fori_loop(0, tm // rc, row_chunk, 0)
    t = _layer_norm(y_ref[...] + bdw_ref[...], cg_ref[...], cb_ref[...])
    t = (t * jax.nn.sigmoid(t)).astype(BF16)
    y = _dot(t, w2_ref[...]) + b2_ref[...]
    z = alpha * x_ref[...] + g_ref[...] * y
    out_ref[...] = _layer_norm(z, lg_ref[...], lb_ref[...])


def _conv(u, w_dw8, b_dw, cg, cb, w2, b2, slot, x, ada, layer, ctx_row, ln_g, ln_b, alpha, tm):
    bsz, l, d = x.shape
    nh = tm // HALO
    last = l // HALO - 1
    tok = pl.BlockSpec((None, tm, d), lambda i, j: (i, j, 0))
    prev = pl.BlockSpec((None, HALO, d), lambda i, j: (i, jnp.maximum(j * nh - 1, 0), 0))
    nxt = pl.BlockSpec((None, HALO, d), lambda i, j: (i, jnp.minimum((j + 1) * nh, last), 0))
    srow = _layer_block(slot, (1, d))
    lrow = _layer_block(layer, (1, d))
    return pl.pallas_call(
        functools.partial(_conv_kernel, alpha=alpha, rc=64, lc=128),
        grid=(bsz, l // tm),
        in_specs=[prev, tok, nxt, _layer_block(slot, (CONV_K, 8, d)), srow, srow, srow,
                  _layer_block(slot, (d, d)), srow, tok, _mod_block(layer, 2, d, ctx_row), lrow, lrow],
        out_specs=tok,
        out_shape=jax.ShapeDtypeStruct((bsz, l, d), F32),
        scratch_shapes=[pltpu.VMEM((tm + 2 * HALO, d), F32), pltpu.VMEM((tm, d), F32)],
        compiler_params=_params(2),
        name="dwconv_pw2_deepnorm",
    )(u, u, u, w_dw8, b_dw, cg, cb, w2, b2, x, ada, ln_g, ln_b)


def kernel(x, c, ctx, c_ctx, w_ada, b_ada, ln_mix_g, ln_mix_b, ln_ffn_g, ln_ffn_b, attn_w_qkv, attn_b_qkv, attn_w_o, attn_b_o, attn_rpb, conv_w_pw1, conv_b_pw1, conv_w_dw, conv_b_dw, conv_ln_g, conv_ln_b, conv_w_pw2, conv_b_pw2, ffn_w1, ffn_w3, ffn_w2):
    bsz, n, d = x.shape
    cl = ctx.shape[1]
    depth = w_ada.shape[0]
    rows = n // GRID_W
    assert d == N_HEADS * HEAD_DIM and HEADS_PER_STEP == 2 and bsz < ADA_ROWS
    assert n % (GROUPS_PER_STEP * Q_ROWS * GRID_W) == 0 and rows >= 2 * KEY_ROWS
    assert cl <= KEY_TILE and (cl + KEY_ROWS * GRID_W) % KEY_TILE == 0
    alpha = (2 * depth) ** 0.25
    last_attn = max(i for i in range(depth) if i % N_MIXERS == 0)
    tm = min(512, n)
    tmc = min(512, cl)
    ctx_row = bsz

    s_rows = jnp.zeros((ADA_ROWS, d), F32).at[:bsz].set(c).at[ctx_row].set(c_ctx)
    ada = _ada_all(s_rows, w_ada, b_ada).reshape(depth, ADA_ROWS, 1, N_MODS * d)

    bf = lambda w: w.astype(BF16)
    rows3 = lambda v: v.reshape(v.shape[0], 1, v.shape[1])
    w1, w3, w2 = bf(ffn_w1), bf(ffn_w3), bf(ffn_w2)
    wqkv, wo, wp1, wp2 = bf(attn_w_qkv), bf(attn_w_o), bf(conv_w_pw1), bf(conv_w_pw2)
    bqkv, bo, bp1, bp2 = rows3(attn_b_qkv), rows3(attn_b_o), rows3(conv_b_pw1), rows3(conv_b_pw2)
    lmg, lmb, lfg, lfb = rows3(ln_mix_g), rows3(ln_mix_b), rows3(ln_ffn_g), rows3(ln_ffn_b)
    conv_args = (jnp.broadcast_to(conv_w_dw[:, :, None, :], conv_w_dw.shape[:2] + (8, d)),
                 rows3(conv_b_dw), rows3(conv_ln_g), rows3(conv_ln_b), wp2, bp2)
    bias = _bias_tables(attn_rpb, rows)

    flat = lambda t: t.reshape(1, bsz * cl, d)
    unflat = lambda t: t.reshape(bsz, cl, d)
    tmf = min(512, bsz * cl)

    xc = ctx
    for i in range(depth):
        slot = i // N_MIXERS
        ctx_live = i < last_attn
        if i % N_MIXERS == 0:
            q, k, v = _qkv(x, ada, i, None, wqkv, bqkv, slot, tm)
            qc, kc, vc = [unflat(t) for t in _qkv(flat(xc), ada, i, ctx_row, wqkv, bqkv, slot, tmf)]
            proj = (_nattn(q, k, v, kc, vc, bias, slot), wo, bo, slot, lmg, lmb)
            proj_c = (flat(_cattn(qc, kc, vc)), wo, bo, slot, lmg, lmb) if ctx_live else None
        else:
            proj = proj_c = None
            u = _glu(x, ada, i, None, wp1, bp1, slot, tm)
            x = _conv(u, *conv_args, slot, x, ada, i, None, lmg, lmb, alpha, tm)
            if ctx_live:
                uc = unflat(_glu(flat(xc), ada, i, ctx_row, wp1, bp1, slot, tmf))
                xc = _conv(uc, *conv_args, slot, xc, ada, i, ctx_row, lmg, lmb, alpha, tmc)
        x = _ffn(x, ada, i, None, w1, w3, w2, lfg, lfb, alpha, tm, proj)
        if ctx_live:
            xc = unflat(_ffn(flat(xc), ada, i, ctx_row, w1, w3, w2, lfg, lfb, alpha, tmf, proj_c))
    return x
```
